```python
import jax, jax.numpy as jnp
from jax import lax
import numpy as np

D_MODEL = 4096
BATCH = 4
SEQ = 2048
DEPTH = 1
DEC_BATCH = 128
DEC_SEQ = 1
PAST_LEN = 2048
PAGE_SIZE = 128

HEAD_DIM = 128
N_HEADS = D_MODEL // HEAD_DIM
FOX_HEADS = N_HEADS // 2
MOBA_HEADS = N_HEADS - FOX_HEADS
KV_HEADS = 4
FOX_GROUP = FOX_HEADS // KV_HEADS
MOBA_GROUP = MOBA_HEADS // KV_HEADS
FOX_WIDTH = FOX_HEADS * HEAD_DIM
MOBA_WIDTH = MOBA_HEADS * HEAD_DIM
KV_WIDTH = KV_HEADS * HEAD_DIM
IN_WIDTH = FOX_WIDTH + 2 * KV_WIDTH + MOBA_WIDTH + 2 * KV_WIDTH + FOX_HEADS
D_FF = ((8 * D_MODEL // 3 + 255) // 256) * 256
MOBA_BLOCK = 256
MOBA_TOPK = 3
FOX_Q_BLOCK = 128
MOBA_Q_BLOCK = 64
ROPE_THETA = 10000.0
RMS_EPS = 1e-6
FORGET_BIAS_INIT = 3.0

kernel_name = "fox_moba_parallel_heads_macaron_decode_step"


def _rmsnorm(x, g):
    xf = x.astype(jnp.float32)
    y = xf * lax.rsqrt(jnp.mean(xf * xf, axis=-1, keepdims=True) + RMS_EPS)
    return (y * g.astype(jnp.float32)).astype(x.dtype)


def _swiglu(x, w_gate, w_up, w_down):
    return (jax.nn.silu(x @ w_gate) * (x @ w_up)) @ w_down


def _half_ffn(h, g_pre, g_post, w_gate, w_up, w_down):
    return h + 0.5 * _rmsnorm(_swiglu(_rmsnorm(h, g_pre), w_gate, w_up, w_down), g_post)


def _rope(x, pos):
    half = HEAD_DIM // 2
    inv_freq = ROPE_THETA ** (-jnp.arange(half, dtype=jnp.float32) / half)
    ang = pos.astype(jnp.float32)[:, None] * inv_freq[None, :]
    cos = jnp.cos(ang)[None, :, None, :]
    sin = jnp.sin(ang)[None, :, None, :]
    xf = x.astype(jnp.float32)
    x1, x2 = xf[..., :half], xf[..., half:]
    return jnp.concatenate([x1 * cos - x2 * sin, x2 * cos + x1 * sin], axis=-1).astype(x.dtype)


def _project(h, g_pre, w_in, b_forget):
    B, S, _ = h.shape
    p = _rmsnorm(h, g_pre) @ w_in
    widths = [FOX_WIDTH, KV_WIDTH, KV_WIDTH, MOBA_WIDTH, KV_WIDTH, KV_WIDTH]
    splits = [int(s) for s in np.cumsum(widths)]
    q_f, k_f, v_f, q_m, k_m, v_m, f_logit = jnp.split(p, splits, axis=-1)
    logf = jax.nn.log_sigmoid((f_logit + b_forget).astype(jnp.float32))
    return (q_f.reshape(B, S, FOX_HEADS, HEAD_DIM), k_f.reshape(B, S, KV_HEADS, HEAD_DIM),
            v_f.reshape(B, S, KV_HEADS, HEAD_DIM), q_m.reshape(B, S, MOBA_HEADS, HEAD_DIM),
            k_m.reshape(B, S, KV_HEADS, HEAD_DIM), v_m.reshape(B, S, KV_HEADS, HEAD_DIM), logf)


def _fox_attend(q, k, v, c_q, c_k, q_pos):
    B, Sq = q.shape[:2]
    L = k.shape[1]
    qg = q.reshape(B, Sq, KV_HEADS, FOX_GROUP, HEAD_DIM)
    s = jnp.einsum("bqkgd,bskd->bkgqs", qg, k).astype(jnp.float32) * (HEAD_DIM ** -0.5)
    cq = c_q.reshape(B, Sq, KV_HEADS, FOX_GROUP).transpose(0, 2, 3, 1)[..., :, None]
    ck = c_k.reshape(B, L, KV_HEADS, FOX_GROUP).transpose(0, 2, 3, 1)[..., None, :]
    s = s + (cq - ck)
    causal = jnp.arange(L)[None, :] <= q_pos[:, None]
    s = jnp.where(causal, s, -jnp.inf)
    p = jax.nn.softmax(s, axis=-1).astype(v.dtype)
    o = jnp.einsum("bkgqs,bskd->bqkgd", p, v)
    return o.reshape(B, Sq, FOX_HEADS, HEAD_DIM)


def _moba_blocks(k, v):
    B, L = k.shape[:2]
    nb = -(-L // MOBA_BLOCK)
    pad = nb * MOBA_BLOCK - L
    def blk(a):
        a = jnp.pad(a, ((0, 0), (0, pad), (0, 0), (0, 0)))
        return a.reshape(B, nb, MOBA_BLOCK, KV_HEADS, HEAD_DIM).transpose(0, 3, 1, 2, 4)
    kb, vb = blk(k), blk(v)
    kmean = jnp.mean(kb.astype(jnp.float32), axis=3).astype(k.dtype)
    return kb, vb, kmean


def _moba_attend(q, kb, vb, kmean, q_pos):
    B, Sq = q.shape[:2]
    nb = kb.shape[2]
    scale = HEAD_DIM ** -0.5
    own = q_pos // MOBA_BLOCK
    qg = q.reshape(B, Sq, KV_HEADS, MOBA_GROUP, HEAD_DIM)
    gate = jnp.einsum("bqkgd,bknd->bqkgn", qg, kmean).astype(jnp.float32).reshape(B, Sq, MOBA_HEADS, nb)
    fully_past = jnp.arange(nb)[None, :] < own[:, None]
    gate = jnp.where(fully_past[None, :, None, :], gate, -jnp.inf)
    n_sel = min(MOBA_TOPK, nb)
    sel_val, sel_idx = lax.top_k(gate, n_sel)
    sel_ok = jnp.isfinite(sel_val)
    bi = jnp.arange(B)[:, None, None, None]
    hk = (jnp.arange(MOBA_HEADS) // MOBA_GROUP)[None, None, :, None]
    k_sel = kb[bi, hk, sel_idx]
    v_sel = vb[bi, hk, sel_idx]
    s_sel = jnp.einsum("bqhd,bqhnjd->bqhnj", q, k_sel).astype(jnp.float32) * scale
    s_sel = jnp.where(sel_ok[..., None], s_sel, -jnp.inf).reshape(B, Sq, MOBA_HEADS, n_sel * MOBA_BLOCK)
    bo = jnp.arange(B)[:, None, None]
    ko = jnp.arange(KV_HEADS)[None, None, :]
    k_own = kb[bo, ko, own[None, :, None]]
    v_own = vb[bo, ko, own[None, :, None]]
    s_own = jnp.einsum("bqkgd,bqkjd->bqkgj", qg, k_own).astype(jnp.float32) * scale
    s_own = s_own.reshape(B, Sq, MOBA_HEADS, MOBA_BLOCK)
    own_pos = own[:, None] * MOBA_BLOCK + jnp.arange(MOBA_BLOCK)[None, :]
    s_own = jnp.where((own_pos <= q_pos[:, None])[None, :, None, :], s_own, -jnp.inf)
    p = jax.nn.softmax(jnp.concatenate([s_sel, s_own], axis=-1), axis=-1).astype(vb.dtype)
    p_sel = p[..., :n_sel * MOBA_BLOCK].reshape(B, Sq, MOBA_HEADS, n_sel, MOBA_BLOCK)
    p_own = p[..., n_sel * MOBA_BLOCK:].reshape(B, Sq, KV_HEADS, MOBA_GROUP, MOBA_BLOCK)
    o = jnp.einsum("bqhnj,bqhnjd->bqhd", p_sel, v_sel)
    o = o + jnp.einsum("bqkgj,bqkjd->bqkgd", p_own, v_own).reshape(B, Sq, MOBA_HEADS, HEAD_DIM)
    return o


def _blockwise(fn, block, seq_arrays, pos):
    S = pos.shape[0]
    n = S // block
    def to_chunks(a):
        return jnp.moveaxis(a.reshape(a.shape[0], n, block, *a.shape[2:]), 1, 0)
    xs = tuple(to_chunks(a) for a in seq_arrays) + (pos.reshape(n, block),)
    out = jnp.moveaxis(lax.map(lambda c: fn(*c), xs), 0, 1)
    return out.reshape(out.shape[0], S, *out.shape[3:])


def _merge(h, o_fox, o_moba, g_out_fox, g_out_moba, w_out, g_post):
    B, S = o_fox.shape[:2]
    z = jnp.concatenate([_rmsnorm(o_fox.reshape(B, S, FOX_WIDTH), g_out_fox),
                         _rmsnorm(o_moba.reshape(B, S, MOBA_WIDTH), g_out_moba)], axis=-1)
    return h + _rmsnorm(z @ w_out, g_post)


def _gather_pages(pool, page_table):
    g = pool[page_table]
    return g.reshape(g.shape[0], g.shape[1] * g.shape[2], *g.shape[3:])


def setup_inputs(seed: int = 0) -> dict:
    key = jax.random.key(seed)
    ks = jax.random.split(key, 26)
    f32 = jnp.float32
    n_pages = PAST_LEN // PAGE_SIZE
    n_used = DEC_BATCH * n_pages
    n_pool = n_used + n_used // 4
    def nrm(k, shape, scale=1.0):
        return scale * jax.random.normal(k, shape, f32)
    def gain(k, shape):
        return 1.0 + 0.05 * jax.random.normal(k, shape, f32)
    kv_shape = (DEPTH, n_pool, PAGE_SIZE, KV_HEADS, HEAD_DIM)
    page_table = jax.random.permutation(ks[7], n_pool)[:n_used].reshape(DEC_BATCH, n_pages).astype(jnp.int32)
    sd, sf = D_MODEL ** -0.5, D_FF ** -0.5
    return {
        "x_prompt": nrm(ks[0], (BATCH, SEQ, D_MODEL)),
        "x_sample": nrm(ks[1], (DEC_BATCH, DEC_SEQ, D_MODEL)),
        "cache_fox_k": nrm(ks[2], kv_shape),
        "cache_fox_v": nrm(ks[3], kv_shape),
        "cache_fox_logf": jax.nn.log_sigmoid(FORGET_BIAS_INIT + nrm(ks[4], (DEPTH, n_pool, PAGE_SIZE, FOX_HEADS))),
        "cache_moba_k": nrm(ks[5], kv_shape),
        "cache_moba_v": nrm(ks[6], kv_shape),
        "page_table": page_table,
        "g_pre_ffn1": gain(ks[8], (DEPTH, D_MODEL)),
        "g_post_ffn1": gain(ks[9], (DEPTH, D_MODEL)),
        "w1_gate": nrm(ks[10], (DEPTH, D_MODEL, D_FF), sd),
        "w1_up": nrm(ks[11], (DEPTH, D_MODEL, D_FF), sd),
        "w1_down": nrm(ks[12], (DEPTH, D_FF, D_MODEL), sf),
        "g_pre_mix": gain(ks[13], (DEPTH, D_MODEL)),
        "w_in": nrm(ks[14], (DEPTH, D_MODEL, IN_WIDTH), sd),
        "b_forget": FORGET_BIAS_INIT + nrm(ks[15], (DEPTH, FOX_HEADS), 0.1),
        "g_out_fox": gain(ks[16], (DEPTH, FOX_WIDTH)),
        "g_out_moba": gain(ks[17], (DEPTH, MOBA_WIDTH)),
        "w_out": nrm(ks[18], (DEPTH, D_MODEL, D_MODEL), sd),
        "g_post_mix": gain(ks[19], (DEPTH, D_MODEL)),
        "g_pre_ffn2": gain(ks[20], (DEPTH, D_MODEL)),
        "g_post_ffn2": gain(ks[21], (DEPTH, D_MODEL)),
        "w2_gate": nrm(ks[22], (DEPTH, D_MODEL, D_FF), sd),
        "w2_up": nrm(ks[23], (DEPTH, D_MODEL, D_FF), sd),
        "w2_down": nrm(ks[24], (DEPTH, D_FF, D_MODEL), sf),
    }


def reference(x_prompt, x_sample, cache_fox_k, cache_fox_v, cache_fox_logf, cache_moba_k, cache_moba_v,
              page_table, g_pre_ffn1, g_post_ffn1, w1_gate, w1_up, w1_down, g_pre_mix, w_in, b_forget,
              g_out_fox, g_out_moba, w_out, g_post_mix, g_pre_ffn2, g_post_ffn2, w2_gate, w2_up, w2_down):
    prompt_pos = jnp.arange(x_prompt.shape[1], dtype=jnp.int32)
    past_len = page_table.shape[1] * PAGE_SIZE
    sample_pos = past_len + jnp.arange(x_sample.shape[1], dtype=jnp.int32)
    hp, hs = x_prompt, x_sample
    pfk, pfv, pfl, pmk, pmv = [], [], [], [], []
    sfk, sfv, sfl, smk, smv = [], [], [], [], []
    for l in range(DEPTH):
        ffn1 = (g_pre_ffn1[l], g_post_ffn1[l], w1_gate[l], w1_up[l], w1_down[l])
        ffn2 = (g_pre_ffn2[l], g_post_ffn2[l], w2_gate[l], w2_up[l], w2_down[l])
        mix_in = (g_pre_mix[l], w_in[l], b_forget[l])
        mix_out = (g_out_fox[l], g_out_moba[l], w_out[l], g_post_mix[l])

        h = _half_ffn(hp, *ffn1)
        q_f, k_f, v_f, q_m, k_m, v_m, logf = _project(h, *mix_in)
        c = jnp.cumsum(logf, axis=1)
        o_f = _blockwise(lambda qc, cc, pc: _fox_attend(qc, k_f, v_f, cc, c, pc),
                         FOX_Q_BLOCK, (q_f, c), prompt_pos)
        q_m = _rope(q_m, prompt_pos)
        k_m = _rope(k_m, prompt_pos)
        kb, vb, km = _moba_blocks(k_m, v_m)
        o_m = _blockwise(lambda qc, pc: _moba_attend(qc, kb, vb, km, pc),
                         MOBA_Q_BLOCK, (q_m,), prompt_pos)
        hp = _half_ffn(_merge(h, o_f, o_m, *mix_out), *ffn2)
        pfk.append(k_f); pfv.append(v_f); pfl.append(logf); pmk.append(k_m); pmv.append(v_m)

        h = _half_ffn(hs, *ffn1)
        q_f, k_f, v_f, q_m, k_m, v_m, logf = _project(h, *mix_in)
        kf_all = jnp.concatenate([_gather_pages(cache_fox_k[l], page_table), k_f], axis=1)
        vf_all = jnp.concatenate([_gather_pages(cache_fox_v[l], page_table), v_f], axis=1)
        lf_all = jnp.concatenate([_gather_pages(cache_fox_logf[l], page_table).astype(jnp.float32), logf], axis=1)
        c = jnp.cumsum(lf_all, axis=1)
        o_f = _fox_attend(q_f, kf_all, vf_all, c[:, past_len:], c, sample_pos)
        q_m = _rope(q_m, sample_pos)
        k_m = _rope(k_m, sample_pos)
        km_all = jnp.concatenate([_gather_pages(cache_moba_k[l], page_table), k_m], axis=1)
        vm_all = jnp.concatenate([_gather_pages(cache_moba_v[l], page_table), v_m], axis=1)
        kb, vb, km = _moba_blocks(km_all, vm_all)
        o_m = _moba_attend(q_m, kb, vb, km, sample_pos)
        hs = _half_ffn(_merge(h, o_f, o_m, *mix_out), *ffn2)
        sfk.append(k_f); sfv.append(v_f); sfl.append(logf); smk.append(k_m); smv.append(v_m)

    return (hp, hs, jnp.stack(pfk), jnp.stack(pfv), jnp.stack(pfl), jnp.stack(pmk), jnp.stack(pmv),
            jnp.stack(sfk), jnp.stack(sfv), jnp.stack(sfl), jnp.stack(smk), jnp.stack(smv))
```

```python
import functools

import jax
import jax.numpy as jnp
from jax import lax
from jax.experimental import pallas as pl
from jax.experimental.pallas import tpu as pltpu

f32 = jnp.float32
bf16 = jnp.bfloat16

MOBA_BLOCK = 256
MOBA_TOPK = 3
ROPE_THETA = 10000.0
RMS_EPS = 1e-6
LANES = 128
ROW_TILE = 256
MM_ROWS = 1024
MM_ROWS_WIDE_K = 512
MM_COLS = 256
DEC_PAGES_PER_STEP = 4
VMEM_CAP = 60 * 1024 * 1024

_NT = (((1,), (1,)), ((), ()))


def _nbytes(shape, dtype):
    n = 1
    for s in shape:
        n *= s
    return n * jnp.dtype(dtype).itemsize


def _params(sem, pipelined, resident=0):
    est = 2 * sum(_nbytes(s, d) for s, d in pipelined) + resident
    limit = min(VMEM_CAP, max(32 * 1024 * 1024, est + est // 4))
    return pltpu.CompilerParams(dimension_semantics=sem, vmem_limit_bytes=int(limit))


def _rmsnorm(x, g):
    ms = jnp.mean(x * x, axis=-1, keepdims=True)
    return x * lax.rsqrt(ms + RMS_EPS) * g


def _prenorm_kernel(x_ref, g_ref, o_ref):
    o_ref[...] = _rmsnorm(x_ref[...], g_ref[...]).astype(o_ref.dtype)


def _prenorm(x, g):
    t, d = x.shape
    tr = min(t, ROW_TILE)
    return pl.pallas_call(
        _prenorm_kernel,
        grid=(t // tr,),
        in_specs=[pl.BlockSpec((tr, d), lambda i: (i, 0)), pl.BlockSpec((1, d), lambda i: (0, 0))],
        out_specs=pl.BlockSpec((tr, d), lambda i: (i, 0)),
        out_shape=jax.ShapeDtypeStruct((t, d), bf16),
        compiler_params=_params(("parallel",), [((tr, d), f32), ((tr, d), bf16)], 2 * _nbytes((tr, d), f32)),
        name="prenorm",
    )(x, g.reshape(1, d))


def _resnorm_kernel(y_ref, r_ref, gp_ref, *rest, weight, has_next):
    y = _rmsnorm(y_ref[...], gp_ref[...])
    if weight != 1.0:
        y = weight * y
    h = r_ref[...] + y
    if has_next:
        gn_ref, h_ref, hn_ref = rest
        h_ref[...] = h
        hn_ref[...] = _rmsnorm(h, gn_ref[...]).astype(hn_ref.dtype)
    else:
        (h_ref,) = rest
        h_ref[...] = h


def _resnorm(y, res, g_post, weight, g_next=None):
    t, d = y.shape
    tr = min(t, ROW_TILE)
    row = pl.BlockSpec((tr, d), lambda i: (i, 0))
    vec = pl.BlockSpec((1, d), lambda i: (0, 0))
    has_next = g_next is not None
    in_specs = [row, row, vec] + ([vec] if has_next else [])
    args = [y, res, g_post.reshape(1, d)] + ([g_next.reshape(1, d)] if has_next else [])
    out_shape = [jax.ShapeDtypeStruct((t, d), f32)] + ([jax.ShapeDtypeStruct((t, d), bf16)] if has_next else [])
    out = pl.pallas_call(
        functools.partial(_resnorm_kernel, weight=weight, has_next=has_next),
        grid=(t // tr,),
        in_specs=in_specs,
        out_specs=[row] * len(out_shape),
        out_shape=out_shape,
        compiler_params=_params(("parallel",), [((tr, d), f32)] * 4, 3 * _nbytes((tr, d), f32)),
        name="resnorm",
    )(*args)
    return (out[0], out[1]) if has_next else (out[0], None)


def _groupnorm_kernel(of_ref, om_ref, gf_ref, gm_ref, z_ref):
    fw = of_ref.shape[1]
    z_ref[:, :fw] = _rmsnorm(of_ref[...], gf_ref[...]).astype(z_ref.dtype)
    z_ref[:, fw:] = _rmsnorm(om_ref[...], gm_ref[...]).astype(z_ref.dtype)


def _groupnorm(o_f, o_m, g_f, g_m):
    t, fw = o_f.shape
    mw = o_m.shape[1]
    tr = min(t, ROW_TILE)
    return pl.pallas_call(
        _groupnorm_kernel,
        grid=(t // tr,),
        in_specs=[pl.BlockSpec((tr, fw), lambda i: (i, 0)), pl.BlockSpec((tr, mw), lambda i: (i, 0)),
                  pl.BlockSpec((1, fw), lambda i: (0, 0)), pl.BlockSpec((1, mw), lambda i: (0, 0))],
        out_specs=pl.BlockSpec((tr, fw + mw), lambda i: (i, 0)),
        out_shape=jax.ShapeDtypeStruct((t, fw + mw), bf16),
        compiler_params=_params(("parallel",), [((tr, fw + mw), f32), ((tr, fw + mw), bf16)],
                                2 * _nbytes((tr, fw + mw), f32)),
        name="groupnorm",
    )(o_f, o_m, g_f.reshape(1, fw), g_m.reshape(1, mw))


def _postproj_kernel(p_ref, cos_ref, sin_ref, b_ref, qf_ref, kf_ref, vf_ref, qm_ref, km_ref, vm_ref, lf_ref,
                     *, fw, kvw, mw, fh, hd):
    cosf = cos_ref[...]
    sinf = sin_ref[...]

    def rope(x):
        return x * cosf + pltpu.roll(x, hd // 2, 1) * sinf

    o = 0
    qf_ref[...] = p_ref[:, o:o + fw].astype(qf_ref.dtype)
    o += fw
    kf_ref[...] = p_ref[:, o:o + kvw]
    o += kvw
    vf_ref[...] = p_ref[:, o:o + kvw]
    o += kvw
    for h in range(mw // hd):
        qm_ref[:, h * hd:(h + 1) * hd] = rope(p_ref[:, o + h * hd:o + (h + 1) * hd]).astype(qm_ref.dtype)
    o += mw
    for h in range(kvw // hd):
        km_ref[:, h * hd:(h + 1) * hd] = rope(p_ref[:, o + h * hd:o + (h + 1) * hd])
    o += kvw
    vm_ref[...] = p_ref[:, o:o + kvw]
    o += kvw
    lf_ref[...] = jax.nn.log_sigmoid(p_ref[:, o:o + fh] + b_ref[...])


def _postproj(p, cosf, sinf, b_forget, *, fw, kvw, mw, fh, hd):
    t, inw = p.shape
    tr = min(t, ROW_TILE)
    row = lambda w: pl.BlockSpec((tr, w), lambda i: (i, 0))
    widths = [(fw, bf16), (kvw, f32), (kvw, f32), (mw, bf16), (kvw, f32), (kvw, f32), (fh, f32)]
    return pl.pallas_call(
        functools.partial(_postproj_kernel, fw=fw, kvw=kvw, mw=mw, fh=fh, hd=hd),
        grid=(t // tr,),
        in_specs=[row(inw), row(hd), row(hd), pl.BlockSpec((1, fh), lambda i: (0, 0))],
        out_specs=[row(w) for w, _ in widths],
        out_shape=[jax.ShapeDtypeStruct((t, w), dt) for w, dt in widths],
        compiler_params=_params(("parallel",), [((tr, inw), f32), ((tr, inw), f32)], _nbytes((tr, inw), f32)),
        name="postproj",
    )(p, cosf, sinf, b_forget.reshape(1, fh))


def _cumsum_kernel(lf_ref, c_ref, *, group, blk):
    x = lf_ref[0].T
    lane = lax.broadcasted_iota(jnp.int32, x.shape, 1)
    sh = 1
    while sh < x.shape[1]:
        x = x + jnp.where(lane >= sh, pltpu.roll(x, sh, 1), 0.0)
        sh *= 2
    for kv in range(c_ref.shape[1]):
        for j in range(c_ref.shape[2]):
            c_ref[0, kv, j] = x[kv * group:(kv + 1) * group, j * blk:(j + 1) * blk]


def _cumsum_logf(logf, *, kvh, blk):
    b, s, fh = logf.shape
    g = fh // kvh
    return pl.pallas_call(
        functools.partial(_cumsum_kernel, group=g, blk=blk),
        grid=(b,),
        in_specs=[pl.BlockSpec((1, s, fh), lambda i: (i, 0, 0))],
        out_specs=pl.BlockSpec((1, kvh, s // blk, g, blk), lambda i: (i, 0, 0, 0, 0)),
        out_shape=jax.ShapeDtypeStruct((b, kvh, s // blk, g, blk), f32),
        compiler_params=_params(("parallel",), [((s, LANES), f32), ((fh, s), f32)], 4 * _nbytes((fh, s), f32)),
        name="cumsum_logf",
    )(logf)


def _gate_up_kernel(x_ref, wg_ref, wu_ref, o_ref):
    x = x_ref[...]
    g = jnp.dot(x, wg_ref[...].astype(bf16), preferred_element_type=f32)
    u = jnp.dot(x, wu_ref[...].astype(bf16), preferred_element_type=f32)
    o_ref[...] = (jax.nn.silu(g) * u).astype(o_ref.dtype)


def _gate_up(xn, w_gate, w_up):
    t, d = xn.shape
    ff = w_gate.shape[1]
    tm = min(t, MM_ROWS)
    tn = MM_COLS
    return pl.pallas_call(
        _gate_up_kernel,
        grid=(t // tm, pl.cdiv(ff, tn)),
        in_specs=[pl.BlockSpec((tm, d), lambda i, j: (i, 0)),
                  pl.BlockSpec((d, tn), lambda i, j: (0, j)),
                  pl.BlockSpec((d, tn), lambda i, j: (0, j))],
        out_specs=pl.BlockSpec((tm, tn), lambda i, j: (i, j)),
        out_shape=jax.ShapeDtypeStruct((t, ff), bf16),
        compiler_params=_params(("parallel", "arbitrary"),
                                [((tm, d), bf16), ((d, tn), f32), ((d, tn), f32), ((tm, tn), bf16)],
                                2 * _nbytes((d, tn), bf16) + 4 * _nbytes((tm, tn), f32)),
        name="gate_up",
    )(xn, w_gate, w_up)


def _matmul_kernel(x_ref, w_ref, o_ref):
    o_ref[...] = jnp.dot(x_ref[...], w_ref[...].astype(bf16), preferred_element_type=f32)


def _matmul(x, w):
    t, k = x.shape
    n = w.shape[1]
    tm = min(t, MM_ROWS if k <= 4096 else MM_ROWS_WIDE_K)
    tn = MM_COLS
    return pl.pallas_call(
        _matmul_kernel,
        grid=(t // tm, pl.cdiv(n, tn)),
        in_specs=[pl.BlockSpec((tm, k), lambda i, j: (i, 0)), pl.BlockSpec((k, tn), lambda i, j: (0, j))],
        out_specs=pl.BlockSpec((tm, tn), lambda i, j: (i, j)),
        out_shape=jax.ShapeDtypeStruct((t, n), f32),
        compiler_params=_params(("parallel", "arbitrary"),
                                [((tm, k), bf16), ((k, tn), f32), ((tm, tn), f32)],
                                _nbytes((k, tn), bf16) + 2 * _nbytes((tm, tn), f32)),
        name="matmul",
    )(x, w)


def _stack_heads(q_ref, q4_ref, group, tq, hd):
    q = q_ref[...]
    for g in range(group):
        q4_ref[g * tq:(g + 1) * tq, :] = q[:, g * hd:(g + 1) * hd]


def _online_update(s, vb, m_ref, l_ref, acc_ref):
    m_prev = m_ref[...]
    m_new = jnp.maximum(m_prev, jnp.max(s, axis=1, keepdims=True))
    alpha = jnp.exp(m_prev - m_new)
    p = jnp.exp(s - m_new)
    l_ref[...] = alpha * l_ref[...] + jnp.sum(p, axis=1, keepdims=True)
    acc_ref[...] = alpha * acc_ref[...] + jnp.dot(p.astype(bf16), vb, preferred_element_type=f32)
    m_ref[...] = m_new


def _causal_mask(s3):
    ri = lax.broadcasted_iota(jnp.int32, s3.shape, 1)
    ci = lax.broadcasted_iota(jnp.int32, s3.shape, 2)
    return jnp.where(ci <= ri, s3, -jnp.inf)


def _unstack_out(o_ref, acc_ref, l_ref, group, tq, hd):
    o = acc_ref[...] / l_ref[...]
    for g in range(group):
        o_ref[:, g * hd:(g + 1) * hd] = o[g * tq:(g + 1) * tq, :]


def _fox_kernel(q_ref, k_ref, v_ref, c_ref, o_ref, q4_ref, m_ref, l_ref, acc_ref, *, group, tq, scale):
    qi = pl.program_id(2)
    hd = k_ref.shape[-1]
    _stack_heads(q_ref, q4_ref, group, tq, hd)
    m_ref[...] = jnp.full(m_ref.shape, -jnp.inf, f32)
    l_ref[...] = jnp.zeros(l_ref.shape, f32)
    acc_ref[...] = jnp.zeros(acc_ref.shape, f32)
    c_here = c_ref[0, 0, qi][:, 0:1]

    def step(j, diagonal):
        off = pl.multiple_of(j * tq, tq)
        kb = k_ref[pl.ds(off, tq), :].astype(bf16)
        vb = v_ref[pl.ds(off, tq), :].astype(bf16)
        s = lax.dot_general(q4_ref[...], kb, _NT, preferred_element_type=f32) * scale
        bias = c_here - c_ref[0, 0, j]
        s3 = s.reshape(group, tq, tq) + bias[:, None, :]
        if diagonal:
            s3 = _causal_mask(s3)
        _online_update(s3.reshape(group * tq, tq), vb, m_ref, l_ref, acc_ref)

    def body(j, carry):
        step(j, False)
        return carry

    lax.fori_loop(0, qi, body, 0)
    step(qi, True)
    _unstack_out(o_ref, acc_ref, l_ref, group, tq, hd)


MAX_MOBA_BLOCKS = LANES // 2


def _rank_select(gate, n_valid, nblk):
    lane = lax.broadcasted_iota(jnp.int32, gate.shape, 1)
    valid = lane < n_valid
    g = jnp.where(valid, gate, -jnp.inf)
    rank = jnp.zeros(gate.shape, f32)
    for d in range(1, nblk):
        lower = pltpu.roll(g, d, 1)
        upper = pltpu.roll(g, gate.shape[1] - d, 1)
        rank = rank + (lower >= g).astype(f32) + (upper > g).astype(f32)
    return jnp.where(valid & (rank < MOBA_TOPK), 0.0, -jnp.inf)


def _moba_kernel(q_ref, k_ref, v_ref, o_ref, q4_ref, m_ref, l_ref, acc_ref, sel_ref, *, group, tq, nblk, scale):
    qi = pl.program_id(2)
    hd = k_ref.shape[-1]
    _stack_heads(q_ref, q4_ref, group, tq, hd)
    m_ref[...] = jnp.full(m_ref.shape, -jnp.inf, f32)
    l_ref[...] = jnp.zeros(l_ref.shape, f32)
    acc_ref[...] = jnp.zeros(acc_ref.shape, f32)

    means = [jnp.mean(k_ref[n * tq:(n + 1) * tq, :], axis=0, keepdims=True) for n in range(nblk)]
    kmean = jnp.concatenate(means + [jnp.zeros((LANES - nblk, hd), f32)], axis=0).astype(bf16)
    gate = lax.dot_general(q4_ref[...], kmean, _NT, preferred_element_type=f32)
    sel_ref[...] = _rank_select(gate, qi, nblk)

    def step(off, mask_col, diagonal):
        kb = k_ref[pl.ds(off, tq), :].astype(bf16)
        vb = v_ref[pl.ds(off, tq), :].astype(bf16)
        s = lax.dot_general(q4_ref[...], kb, _NT, preferred_element_type=f32) * scale
        if diagonal:
            s = _causal_mask(s.reshape(group, tq, tq)).reshape(group * tq, tq)
        else:
            s = s + mask_col
        _online_update(s, vb, m_ref, l_ref, acc_ref)

    step(pl.multiple_of(qi * tq, tq), None, True)
    for j in range(nblk - 1):
        @pl.when(j < qi)
        def _():
            step(j * tq, sel_ref[:, j:j + 1], False)

    _unstack_out(o_ref, acc_ref, l_ref, group, tq, hd)


def _prompt_attention(kind, q, k, v, c5, *, batch, seq, kvh, hd):
    t, qw = q.shape
    group = qw // (kvh * hd)
    tq = MOBA_BLOCK
    nq = seq // tq
    rows = group * tq
    scale = hd ** -0.5
    q_spec = pl.BlockSpec((tq, group * hd), lambda b, h, i: (b * nq + i, h))
    kv_spec = pl.BlockSpec((seq, hd), lambda b, h, i: (b, h))
    scratch = [pltpu.VMEM((rows, hd), bf16), pltpu.VMEM((rows, 1), f32), pltpu.VMEM((rows, 1), f32),
               pltpu.VMEM((rows, hd), f32)]
    if kind == "fox":
        body = functools.partial(_fox_kernel, group=group, tq=tq, scale=scale)
        in_specs = [q_spec, kv_spec, kv_spec,
                    pl.BlockSpec((1, 1, nq, group, tq), lambda b, h, i: (b, h, 0, 0, 0))]
        args = (q, k, v, c5)
    else:
        assert nq <= MAX_MOBA_BLOCKS
        body = functools.partial(_moba_kernel, group=group, tq=tq, nblk=nq, scale=scale)
        in_specs = [q_spec, kv_spec, kv_spec]
        args = (q, k, v)
        scratch = scratch + [pltpu.VMEM((rows, LANES), f32)]
    return pl.pallas_call(
        body,
        grid=(batch, kvh, nq),
        in_specs=in_specs,
        out_specs=q_spec,
        out_shape=jax.ShapeDtypeStruct((t, qw), f32),
        scratch_shapes=scratch,
        compiler_params=_params(("parallel", "parallel", "arbitrary"),
                                [((tq, group * hd), bf16), ((seq, hd), f32), ((seq, hd), f32),
                                 ((tq, group * hd), f32), ((nq * 8, tq), f32)],
                                4 * _nbytes((rows, LANES), f32) + 6 * _nbytes((rows, tq), f32)),
        name=kind + "_attention",
    )(*args)


def _block_diag_q(q, kvh):
    heads, hd = q.shape
    group = heads // kvh
    qt = jnp.concatenate([q] * kvh, axis=1)
    row_kv = lax.broadcasted_iota(jnp.int32, qt.shape, 0) // group
    col_kv = lax.broadcasted_iota(jnp.int32, qt.shape, 1) // hd
    return jnp.where(row_kv == col_kv, qt, jnp.zeros_like(qt))


def _own_kv_slab(x, kvh):
    heads, w = x.shape
    hd = w // kvh
    group = heads // kvh
    row_kv = lax.broadcasted_iota(jnp.int32, (heads, hd), 0) // group
    out = jnp.zeros((heads, hd), x.dtype)
    for g in range(kvh):
        out = out + jnp.where(row_kv == g, x[:, g * hd:(g + 1) * hd], 0.0)
    return out


def _prefix_lanes(x):
    lane = lax.broadcasted_iota(jnp.int32, x.shape, 1)
    sh = 1
    while sh < x.shape[1]:
        x = x + jnp.where(lane >= sh, pltpu.roll(x, sh, 1), 0.0)
        sh *= 2
    return x


def _page_rows(ref, kvh):
    return jnp.concatenate([ref[0, :, h, :] for h in range(kvh)], axis=1)


def _lane_softmax_update(s, vb, m, l, acc):
    m_new = jnp.maximum(m, jnp.max(s, axis=1, keepdims=True))
    alpha = jnp.exp(m - m_new)
    p = jnp.exp(s - m_new)
    l = alpha * l + jnp.sum(p, axis=1, keepdims=True)
    acc = alpha * acc + jnp.dot(p.astype(bf16), vb, preferred_element_type=f32)
    return m_new, l, acc


def _fox_dec_kernel(pt_ref, q_ref, *refs, pps, kvh, scale):
    k_refs = refs[:pps]
    v_refs = refs[pps:2 * pps]
    lf_refs = refs[2 * pps:3 * pps]
    kn_ref, vn_ref, lfn_ref, o_ref, qbd_ref, m_ref, l_ref, acc_ref, cpre_ref = refs[3 * pps:]
    step = pl.program_id(1)

    @pl.when(step == 0)
    def _():
        qbd_ref[...] = _block_diag_q(q_ref[0], kvh)
        m_ref[...] = jnp.full(m_ref.shape, -jnp.inf, f32)
        l_ref[...] = jnp.zeros(l_ref.shape, f32)
        acc_ref[...] = jnp.zeros(acc_ref.shape, f32)
        cpre_ref[...] = jnp.zeros(cpre_ref.shape, f32)

    qbd = qbd_ref[...]
    m, l, acc, cpre = m_ref[...], l_ref[...], acc_ref[...], cpre_ref[...]
    for i in range(pps):
        kb = _page_rows(k_refs[i], kvh).astype(bf16)
        vb = _page_rows(v_refs[i], kvh).astype(bf16)
        c_in = _prefix_lanes(lf_refs[i][0]) + cpre
        s = lax.dot_general(qbd, kb, _NT, preferred_element_type=f32) * scale - c_in
        m, l, acc = _lane_softmax_update(s, vb, m, l, acc)
        cpre = c_in[:, c_in.shape[1] - 1:]
    m_ref[...], l_ref[...], acc_ref[...], cpre_ref[...] = m, l, acc, cpre

    @pl.when(step == pl.num_programs(1) - 1)
    def _():
        kn = kn_ref[0].astype(bf16).astype(f32)
        vn = vn_ref[0].astype(bf16).astype(f32)
        c_new = cpre + lfn_ref[0]
        s_new = jnp.sum(qbd.astype(f32) * kn, axis=1, keepdims=True) * scale - c_new
        m_new = jnp.maximum(m, s_new)
        alpha = jnp.exp(m - m_new)
        p_new = jnp.exp(s_new - m_new)
        l_fin = alpha * l + p_new
        acc_fin = alpha * acc + p_new * vn
        o_ref[0] = _own_kv_slab(acc_fin / l_fin, kvh)


def _moba_dec_kernel(pt_ref, q_ref, *refs, pps, kvh, ppb, scale):
    k_refs = refs[:pps]
    v_refs = refs[pps:2 * pps]
    kn_ref, vn_ref, o_ref, qbd_ref, m_ref, l_ref, acc_ref, ksum_ref = refs[2 * pps:]
    step = pl.program_id(1)
    nblk = m_ref.shape[0]
    page = k_refs[0].shape[1]

    @pl.when(step == 0)
    def _():
        qbd_ref[...] = _block_diag_q(q_ref[0], kvh)

    qbd = qbd_ref[...]
    for b0 in range(pps // ppb):
        blk = step * (pps // ppb) + b0
        m = jnp.full(m_ref.shape[1:], -jnp.inf, f32)
        l = jnp.zeros(l_ref.shape[1:], f32)
        acc = jnp.zeros(acc_ref.shape[1:], f32)
        ksum = jnp.zeros(ksum_ref.shape[1:], f32)
        for i in range(b0 * ppb, (b0 + 1) * ppb):
            kf = _page_rows(k_refs[i], kvh)
            s = lax.dot_general(qbd, kf.astype(bf16), _NT, preferred_element_type=f32) * scale
            m, l, acc = _lane_softmax_update(s, _page_rows(v_refs[i], kvh).astype(bf16), m, l, acc)
            ksum = ksum + jnp.sum(kf, axis=0, keepdims=True)
        m_ref[blk], l_ref[blk], acc_ref[blk], ksum_ref[blk] = m, l, acc, ksum

    @pl.when(step == pl.num_programs(1) - 1)
    def _():
        ksums = [ksum_ref[n] for n in range(nblk)] + [jnp.zeros((LANES - nblk, acc_ref.shape[2]), f32)]
        kmean = jnp.concatenate(ksums, axis=0) / (ppb * page)
        gate = lax.dot_general(qbd, kmean.astype(bf16), _NT, preferred_element_type=f32)
        sel = _rank_select(gate, nblk, nblk)
        kn = kn_ref[0].astype(bf16).astype(f32)
        vn = vn_ref[0].astype(bf16).astype(f32)
        s_new = jnp.sum(qbd.astype(f32) * kn, axis=1, keepdims=True) * scale
        m_all = s_new
        for n in range(nblk):
            m_all = jnp.maximum(m_all, m_ref[n] + sel[:, n:n + 1])
        p_new = jnp.exp(s_new - m_all)
        l_fin = p_new
        acc_fin = p_new * vn
        for n in range(nblk):
            w = jnp.exp(m_ref[n] + sel[:, n:n + 1] - m_all)
            l_fin = l_fin + w * l_ref[n]
            acc_fin = acc_fin + w * acc_ref[n]
        o_ref[0] = _own_kv_slab(acc_fin / l_fin, kvh)


def _decode_attention(kind, q, k_pool, v_pool, lf_pool, k_new, v_new, lf_new, page_table, *, kvh, hd):
    db, npages = page_table.shape
    heads = q.shape[1] // hd
    kvw = kvh * hd
    page = k_pool.shape[1]
    pps = DEC_PAGES_PER_STEP
    assert npages % pps == 0
    scale = hd ** -0.5
    q3 = q.reshape(db, heads, hd)

    def page_of(b, s, pt, i):
        return pt[b * npages + s * pps + i]

    def paged_kv(i):
        return pl.BlockSpec((1, page, kvh, hd), lambda b, s, pt: (page_of(b, s, pt, i), 0, 0, 0))

    per_seq = lambda shape: pl.BlockSpec((1,) + shape, lambda b, s, pt: (b, 0, 0))
    in_specs = [per_seq((heads, hd))] + [paged_kv(i) for i in range(pps)] * 2
    args = [q3] + [k_pool] * pps + [v_pool] * pps
    scratch = [pltpu.VMEM((heads, kvw), bf16)]
    if kind == "fox":
        lf_t = jnp.swapaxes(lf_pool, 1, 2)
        in_specs += [pl.BlockSpec((1, heads, page), lambda b, s, pt, i=i: (page_of(b, s, pt, i), 0, 0))
                     for i in range(pps)]
        args += [lf_t] * pps
        in_specs += [per_seq((1, kvw)), per_seq((1, kvw)), per_seq((heads, 1))]
        args += [k_new.reshape(db, 1, kvw), v_new.reshape(db, 1, kvw), lf_new.reshape(db, heads, 1)]
        scratch += [pltpu.VMEM((heads, 1), f32), pltpu.VMEM((heads, 1), f32), pltpu.VMEM((heads, kvw), f32),
                    pltpu.VMEM((heads, 1), f32)]
        body = functools.partial(_fox_dec_kernel, pps=pps, kvh=kvh, scale=scale)
    else:
        ppb = MOBA_BLOCK // page
        assert pps % ppb == 0 and npages % ppb == 0 and npages // ppb <= MAX_MOBA_BLOCKS
        nblk = npages // ppb
        in_specs += [per_seq((1, kvw)), per_seq((1, kvw))]
        args += [k_new.reshape(db, 1, kvw), v_new.reshape(db, 1, kvw)]
        scratch += [pltpu.VMEM((nblk, heads, 1), f32), pltpu.VMEM((nblk, heads, 1), f32),
                    pltpu.VMEM((nblk, heads, kvw), f32), pltpu.VMEM((nblk, 1, kvw), f32)]
        body = functools.partial(_moba_dec_kernel, pps=pps, kvh=kvh, ppb=ppb, scale=scale)
    out = pl.pallas_call(
        body,
        grid_spec=pltpu.PrefetchScalarGridSpec(
            num_scalar_prefetch=1,
            grid=(db, npages // pps),
            in_specs=in_specs,
            out_specs=pl.BlockSpec((1, heads, hd), lambda b, s, pt: (b, 0, 0)),
            scratch_shapes=scratch,
        ),
        out_shape=jax.ShapeDtypeStruct((db, heads, hd), f32),
        compiler_params=_params(("parallel", "arbitrary"), [((page, kvw), f32)] * (3 * pps),
                                8 * _nbytes((page, kvw), f32)),
        name=kind + "_decode",
    )(page_table.reshape(-1), *args)
    return out.reshape(db, heads * hd)


def _rope_tables(pos, hd):
    half = hd // 2
    inv_freq = ROPE_THETA ** (-jnp.arange(half, dtype=f32) / half)
    ang = pos.astype(f32)[:, None] * inv_freq[None, :]
    cos, sin = jnp.cos(ang), jnp.sin(ang)
    return jnp.concatenate([cos, cos], axis=-1), jnp.concatenate([-sin, sin], axis=-1)


def _half_ffn(h, hn, g_post, w_gate, w_up, w_down, g_next):
    a = _gate_up(hn, w_gate, w_up)
    y = _matmul(a, w_down)
    return _resnorm(y, h, g_post, 0.5, g_next)


def kernel(x_prompt, x_sample, cache_fox_k, cache_fox_v, cache_fox_logf, cache_moba_k, cache_moba_v, page_table,
           g_pre_ffn1, g_post_ffn1, w1_gate, w1_up, w1_down, g_pre_mix, w_in, b_forget, g_out_fox, g_out_moba,
           w_out, g_post_mix, g_pre_ffn2, g_post_ffn2, w2_gate, w2_up, w2_down):
    depth = w_in.shape[0]
    assert depth == 1, "one layer: the sample group reads the caches the prompt group does not write"
    batch, seq, d = x_prompt.shape
    db, dseq, _ = x_sample.shape
    assert dseq == 1
    kvh, hd = cache_fox_k.shape[3], cache_fox_k.shape[4]
    assert hd == LANES
    kvw = kvh * hd
    fh = b_forget.shape[-1]
    fw = fh * hd
    mw = w_in.shape[-1] - fw - 4 * kvw - fh
    page = cache_fox_k.shape[2]
    past_len = page_table.shape[1] * page
    assert seq % MOBA_BLOCK == 0 and past_len % MOBA_BLOCK == 0

    def layer0(a):
        return a.reshape(a.shape[1:])

    def trunk(x, pos, attend):
        hn = _prenorm(x, layer0(g_pre_ffn1))
        h, hn = _half_ffn(x, hn, layer0(g_post_ffn1), layer0(w1_gate), layer0(w1_up), layer0(w1_down),
                          layer0(g_pre_mix))
        p = _matmul(hn, layer0(w_in))
        cosf, sinf = _rope_tables(pos, hd)
        qf, kf, vf, qm, km, vm, lf = _postproj(p, cosf, sinf, layer0(b_forget), fw=fw, kvw=kvw, mw=mw, fh=fh, hd=hd)
        o_f, o_m = attend(qf, kf, vf, qm, km, vm, lf)
        z = _groupnorm(o_f, o_m, layer0(g_out_fox), layer0(g_out_moba))
        y = _matmul(z, layer0(w_out))
        h, hn = _resnorm(y, h, layer0(g_post_mix), 1.0, layer0(g_pre_ffn2))
        out, _ = _half_ffn(h, hn, layer0(g_post_ffn2), layer0(w2_gate), layer0(w2_up), layer0(w2_down), None)
        return out, (kf, vf, lf, km, vm)

    def attend_prompt(qf, kf, vf, qm, km, vm, lf):
        c5 = _cumsum_logf(lf.reshape(batch, seq, fh), kvh=kvh, blk=MOBA_BLOCK)
        o_f = _prompt_attention("fox", qf, kf, vf, c5, batch=batch, seq=seq, kvh=kvh, hd=hd)
        o_m = _prompt_attention("moba", qm, km, vm, None, batch=batch, seq=seq, kvh=kvh, hd=hd)
        return o_f, o_m

    def attend_sample(qf, kf, vf, qm, km, vm, lf):
        o_f = _decode_attention("fox", qf, layer0(cache_fox_k), layer0(cache_fox_v), layer0(cache_fox_logf),
                                kf, vf, lf, page_table, kvh=kvh, hd=hd)
        o_m = _decode_attention("moba", qm, layer0(cache_moba_k), layer0(cache_moba_v), None, km, vm, None,
                                page_table, kvh=kvh, hd=hd)
        return o_f, o_m

    prompt_pos = jnp.tile(jnp.arange(seq, dtype=jnp.int32), batch)
    sample_pos = jnp.full((db,), past_len, jnp.int32)
    yp, (pfk, pfv, pfl, pmk, pmv) = trunk(x_prompt.reshape(batch * seq, d), prompt_pos, attend_prompt)
    ys, (sfk, sfv, sfl, smk, smv) = trunk(x_sample.reshape(db, d), sample_pos, attend_sample)

    def rows(a, b, s):
        return a.reshape(depth, b, s, kvh, hd)

    return (yp.reshape(batch, seq, d), ys.reshape(db, dseq, d),
            rows(pfk, batch, seq), rows(pfv, batch, seq), pfl.reshape(depth, batch, seq, fh),
            rows(pmk, batch, seq), rows(pmv, batch, seq),
            rows(sfk, db, dseq), rows(sfv, db, dseq), sfl.reshape(depth, db, dseq, fh),
            rows(smk, db, dseq), rows(smv, db, dseq))
```

```python
import functools

import jax
import jax.numpy as jnp
from jax import lax
from jax.experimental import pallas as pl
from jax.experimental.pallas import tpu as pltpu

f32 = jnp.float32
bf16 = jnp.bfloat16

MOBA_BLOCK = 256
MOBA_TOPK = 3
ROPE_THETA = 10000.0
RMS_EPS = 1e-6
LANES = 128
SUBLANES = 8
BF16_ROWS = 16
ROW_TILE = 256
MM_ROWS = 1024
MM_COLS = 256
MM_COLS_WIDE = 512
VMEM_CAP = 60 * 1024 * 1024

_NT = (((1,), (1,)), ((), ()))


def _nbytes(shape, dtype):
    n = 1
    for s in shape:
        n *= s
    return n * jnp.dtype(dtype).itemsize


def _params(sem, pipelined, resident=0):
    est = 2 * sum(_nbytes(s, d) for s, d in pipelined) + resident
    limit = min(VMEM_CAP, max(32 * 1024 * 1024, est + est // 4))
    return pltpu.CompilerParams(dimension_semantics=sem, vmem_limit_bytes=int(limit))


def _rmsnorm(x, g):
    ms = jnp.mean(x * x, axis=-1, keepdims=True)
    return x * lax.rsqrt(ms + RMS_EPS) * g


def _prenorm_kernel(x_ref, g_ref, o_ref):
    o_ref[...] = _rmsnorm(x_ref[...], g_ref[...]).astype(o_ref.dtype)


def _prenorm(x, g):
    t, d = x.shape
    tr = min(t, ROW_TILE)
    return pl.pallas_call(
        _prenorm_kernel,
        grid=(t // tr,),
        in_specs=[pl.BlockSpec((tr, d), lambda i: (i, 0)), pl.BlockSpec((1, d), lambda i: (0, 0))],
        out_specs=pl.BlockSpec((tr, d), lambda i: (i, 0)),
        out_shape=jax.ShapeDtypeStruct((t, d), bf16),
        compiler_params=_params(("parallel",), [((tr, d), f32), ((tr, d), bf16)], 2 * _nbytes((tr, d), f32)),
        name="prenorm",
    )(x, g.reshape(1, d))


def _resnorm_kernel(y_ref, r_ref, gp_ref, *rest, weight, has_next):
    y = _rmsnorm(y_ref[...], gp_ref[...])
    if weight != 1.0:
        y = weight * y
    h = r_ref[...] + y
    if has_next:
        gn_ref, h_ref, hn_ref = rest
        h_ref[...] = h
        hn_ref[...] = _rmsnorm(h, gn_ref[...]).astype(hn_ref.dtype)
    else:
        (h_ref,) = rest
        h_ref[...] = h


def _resnorm(y, res, g_post, weight, g_next=None):
    t, d = y.shape
    tr = min(t, ROW_TILE)
    row = pl.BlockSpec((tr, d), lambda i: (i, 0))
    vec = pl.BlockSpec((1, d), lambda i: (0, 0))
    has_next = g_next is not None
    in_specs = [row, row, vec] + ([vec] if has_next else [])
    args = [y, res, g_post.reshape(1, d)] + ([g_next.reshape(1, d)] if has_next else [])
    out_shape = [jax.ShapeDtypeStruct((t, d), f32)] + ([jax.ShapeDtypeStruct((t, d), bf16)] if has_next else [])
    out = pl.pallas_call(
        functools.partial(_resnorm_kernel, weight=weight, has_next=has_next),
        grid=(t // tr,),
        in_specs=in_specs,
        out_specs=[row] * len(out_shape),
        out_shape=out_shape,
        compiler_params=_params(("parallel",), [((tr, d), f32)] * 4, 3 * _nbytes((tr, d), f32)),
        name="resnorm",
    )(*args)
    return (out[0], out[1]) if has_next else (out[0], None)


def _groupnorm_kernel(of_ref, om_ref, gf_ref, gm_ref, z_ref):
    fw = of_ref.shape[1]
    z_ref[:, :fw] = _rmsnorm(of_ref[...], gf_ref[...]).astype(z_ref.dtype)
    z_ref[:, fw:] = _rmsnorm(om_ref[...], gm_ref[...]).astype(z_ref.dtype)


def _groupnorm(o_f, o_m, g_f, g_m):
    t, fw = o_f.shape
    mw = o_m.shape[1]
    tr = min(t, ROW_TILE)
    return pl.pallas_call(
        _groupnorm_kernel,
        grid=(t // tr,),
        in_specs=[pl.BlockSpec((tr, fw), lambda i: (i, 0)), pl.BlockSpec((tr, mw), lambda i: (i, 0)),
                  pl.BlockSpec((1, fw), lambda i: (0, 0)), pl.BlockSpec((1, mw), lambda i: (0, 0))],
        out_specs=pl.BlockSpec((tr, fw + mw), lambda i: (i, 0)),
        out_shape=jax.ShapeDtypeStruct((t, fw + mw), bf16),
        compiler_params=_params(("parallel",), [((tr, fw + mw), f32), ((tr, fw + mw), bf16)],
                                2 * _nbytes((tr, fw + mw), f32)),
        name="groupnorm",
    )(o_f, o_m, g_f.reshape(1, fw), g_m.reshape(1, mw))


def _postproj_kernel(p_ref, cos_ref, sin_ref, b_ref, qf_ref, kf_ref, vf_ref, qm_ref, km_ref, vm_ref, lf_ref,
                     *, fw, kvw, mw, fh, hd):
    cosf = cos_ref[...]
    sinf = sin_ref[...]

    def rope(x):
        return x * cosf + pltpu.roll(x, hd // 2, 1) * sinf

    o = 0
    qf_ref[...] = p_ref[:, o:o + fw].astype(qf_ref.dtype)
    o += fw
    kf_ref[...] = p_ref[:, o:o + kvw]
    o += kvw
    vf_ref[...] = p_ref[:, o:o + kvw]
    o += kvw
    for h in range(mw // hd):
        qm_ref[:, h * hd:(h + 1) * hd] = rope(p_ref[:, o + h * hd:o + (h + 1) * hd]).astype(qm_ref.dtype)
    o += mw
    for h in range(kvw // hd):
        km_ref[:, h * hd:(h + 1) * hd] = rope(p_ref[:, o + h * hd:o + (h + 1) * hd])
    o += kvw
    vm_ref[...] = p_ref[:, o:o + kvw]
    o += kvw
    lf_ref[...] = jax.nn.log_sigmoid(p_ref[:, o:o + fh] + b_ref[...])


def _postproj(p, cosf, sinf, b_forget, *, fw, kvw, mw, fh, hd):
    t, inw = p.shape
    tr = min(t, ROW_TILE)
    row = lambda w: pl.BlockSpec((tr, w), lambda i: (i, 0))
    widths = [(fw, bf16), (kvw, f32), (kvw, f32), (mw, bf16), (kvw, f32), (kvw, f32), (fh, f32)]
    return pl.pallas_call(
        functools.partial(_postproj_kernel, fw=fw, kvw=kvw, mw=mw, fh=fh, hd=hd),
        grid=(t // tr,),
        in_specs=[row(inw), row(hd), row(hd), pl.BlockSpec((1, fh), lambda i: (0, 0))],
        out_specs=[row(w) for w, _ in widths],
        out_shape=[jax.ShapeDtypeStruct((t, w), dt) for w, dt in widths],
        compiler_params=_params(("parallel",), [((tr, inw), f32), ((tr, inw), f32)], _nbytes((tr, inw), f32)),
        name="postproj",
    )(p, cosf, sinf, b_forget.reshape(1, fh))


def _prefix_lanes(x):
    lane = lax.broadcasted_iota(jnp.int32, x.shape, 1)
    sh = 1
    while sh < x.shape[1]:
        x = x + jnp.where(lane >= sh, pltpu.roll(x, sh, 1), 0.0)
        sh *= 2
    return x


def _cumsum_kernel(lf_ref, c_ref, *, group, inv_scale):
    c = _prefix_lanes(lf_ref[0].T)
    bt = (c * (-inv_scale)).T
    pad = jnp.zeros((bt.shape[0], LANES - 3 * group), f32)
    for kv in range(c_ref.shape[1]):
        b = bt[:, kv * group:(kv + 1) * group]
        hi = b.astype(bf16).astype(f32)
        mid = (b - hi).astype(bf16).astype(f32)
        lo = b - hi - mid
        c_ref[0, kv] = jnp.concatenate([hi, mid, lo, pad], axis=1).astype(bf16)


def _cumsum_logf(logf, *, kvh, scale):
    b, s, fh = logf.shape
    g = fh // kvh
    assert 3 * g <= LANES
    return pl.pallas_call(
        functools.partial(_cumsum_kernel, group=g, inv_scale=1.0 / scale),
        grid=(b,),
        in_specs=[pl.BlockSpec((1, s, fh), lambda i: (i, 0, 0))],
        out_specs=pl.BlockSpec((1, kvh, s, LANES), lambda i: (i, 0, 0, 0)),
        out_shape=jax.ShapeDtypeStruct((b, kvh, s, LANES), bf16),
        compiler_params=_params(("parallel",), [((s, LANES), f32), ((kvh * s, LANES), bf16)],
                                8 * _nbytes((s, LANES), f32)),
        name="cumsum_logf",
    )(logf)


def _gate_up_kernel(x_ref, wg_ref, wu_ref, o_ref):
    x = x_ref[...]
    g = jnp.dot(x, wg_ref[...].astype(bf16), preferred_element_type=f32)
    u = jnp.dot(x, wu_ref[...].astype(bf16), preferred_element_type=f32)
    o_ref[...] = (jax.nn.silu(g) * u).astype(o_ref.dtype)


def _gate_up(xn, w_gate, w_up):
    t, d = xn.shape
    ff = w_gate.shape[1]
    tm = min(t, MM_ROWS)
    tn = MM_COLS
    return pl.pallas_call(
        _gate_up_kernel,
        grid=(t // tm, pl.cdiv(ff, tn)),
        in_specs=[pl.BlockSpec((tm, d), lambda i, j: (i, 0)),
                  pl.BlockSpec((d, tn), lambda i, j: (0, j)),
                  pl.BlockSpec((d, tn), lambda i, j: (0, j))],
        out_specs=pl.BlockSpec((tm, tn), lambda i, j: (i, j)),
        out_shape=jax.ShapeDtypeStruct((t, ff), bf16),
        compiler_params=_params(("parallel", "arbitrary"),
                                [((tm, d), bf16), ((d, tn), f32), ((d, tn), f32), ((tm, tn), bf16)],
                                2 * _nbytes((d, tn), bf16) + 4 * _nbytes((tm, tn), f32)),
        name="gate_up",
    )(xn, w_gate, w_up)


def _matmul_kernel(x_ref, w_ref, o_ref):
    o_ref[...] = jnp.dot(x_ref[...], w_ref[...].astype(bf16), preferred_element_type=f32)


def _matmul(x, w, tn):
    t, k = x.shape
    n = w.shape[1]
    tm = min(t, MM_ROWS)
    return pl.pallas_call(
        _matmul_kernel,
        grid=(t // tm, pl.cdiv(n, tn)),
        in_specs=[pl.BlockSpec((tm, k), lambda i, j: (i, 0), pipeline_mode=pl.Buffered(1)),
                  pl.BlockSpec((k, tn), lambda i, j: (0, j))],
        out_specs=pl.BlockSpec((tm, tn), lambda i, j: (i, j)),
        out_shape=jax.ShapeDtypeStruct((t, n), f32),
        compiler_params=_params(("parallel", "arbitrary"),
                                [((k, tn), f32), ((tm, tn), f32)],
                                _nbytes((tm, k), bf16) + _nbytes((k, tn), bf16) + 2 * _nbytes((tm, tn), f32)),
        name="matmul",
    )(x, w)


MAX_MOBA_BLOCKS = LANES // 2


def _rank_select(gate, n_valid, nblk, axis):
    idx = lax.broadcasted_iota(jnp.int32, gate.shape, axis)
    valid = idx < n_valid
    g = jnp.where(valid, gate, -jnp.inf)
    rank = jnp.zeros(gate.shape, f32)
    size = gate.shape[axis]
    assert size >= 2 * nblk
    for d in range(1, nblk):
        lower = pltpu.roll(g, d, axis)
        upper = pltpu.roll(g, size - d, axis)
        rank = rank + (lower >= g).astype(f32) + (upper > g).astype(f32)
    return jnp.where(valid & (rank < MOBA_TOPK), 0.0, -jnp.inf)


def _attn_kernel(*refs, kind, group, tq, nblk, scale):
    if kind == "fox":
        q_ref, k_ref, v_ref, c_ref, o_ref, kb_ref, vt_ref, qa_ref, m_ref, acc_ref = refs
    else:
        q_ref, k_ref, v_ref, o_ref, kb_ref, vt_ref, qa_ref, m_ref, acc_ref, km_ref, sel_ref = refs
    qi = pl.program_id(2)
    hd = k_ref.shape[-1]
    rows = group * tq

    @pl.when(qi == 0)
    def _():
        for n in range(nblk):
            kn = k_ref[n * tq:(n + 1) * tq, :]
            kb_ref[n, :, :hd] = kn.astype(bf16)
            if kind == "fox":
                kb_ref[n, :, hd:] = c_ref[0, 0, n * tq:(n + 1) * tq, :]
            vt_ref[n, :hd, :] = v_ref[n * tq:(n + 1) * tq, :].T.astype(bf16)
            vt_ref[n, hd:, :] = jnp.ones((BF16_ROWS, tq), bf16)
        if kind == "moba":
            means = [jnp.mean(k_ref[n * tq:(n + 1) * tq, :], axis=0, keepdims=True) for n in range(nblk)]
            km_ref[...] = jnp.concatenate(means + [jnp.zeros((LANES - nblk, hd), f32)], axis=0).astype(bf16)

    q = q_ref[...]
    for g in range(group):
        qa_ref[g * tq:(g + 1) * tq, :hd] = q[:, g * hd:(g + 1) * hd]
    lane_q = lax.broadcasted_iota(jnp.int32, (tq, rows), 1)
    if kind == "fox":
        lane = lax.broadcasted_iota(jnp.int32, (rows, hd), 1)
        head = lax.broadcasted_iota(jnp.int32, (rows, hd), 0) // tq
        pick = (lane == head) | (lane == head + group) | (lane == head + 2 * group)
        qa_ref[:, hd:] = jnp.where(pick, 1.0, 0.0).astype(bf16)
    else:
        gate = lax.dot_general(km_ref[...], qa_ref[...], _NT, preferred_element_type=f32)
        sel_ref[...] = _rank_select(gate[:sel_ref.shape[0]], qi, nblk, 0)

    m_ref[...] = jnp.full(m_ref.shape, -jnp.inf, f32)
    acc_ref[...] = jnp.zeros(acc_ref.shape, f32)

    def step(j, diagonal):
        s = lax.dot_general(kb_ref[j], qa_ref[...], _NT, preferred_element_type=f32) * scale
        if diagonal:
            key = lax.broadcasted_iota(jnp.int32, s.shape, 0)
            s = jnp.where(key <= (lane_q & (tq - 1)), s, -jnp.inf)
        elif kind == "moba":
            s = s + sel_ref[pl.ds(j, 1), :]
        m_prev = m_ref[...]
        m_new = jnp.maximum(m_prev, jnp.max(s, axis=0, keepdims=True))
        alpha = jnp.exp(m_prev - m_new)
        p = jnp.exp(s - m_new).astype(bf16)
        acc_ref[...] = alpha * acc_ref[...] + jnp.dot(vt_ref[j], p, preferred_element_type=f32)
        m_ref[...] = m_new

    step(qi, True)

    def body(i, carry):
        step(2 * i, False)
        step(2 * i + 1, False)
        return carry

    lax.fori_loop(0, lax.shift_right_logical(qi, 1), body, 0)

    @pl.when((qi & 1) == 1)
    def _():
        step(qi - 1, False)

    acc = acc_ref[...]
    o = (acc[:hd] / acc[hd:hd + 1]).T
    for g in range(group):
        o_ref[:, g * hd:(g + 1) * hd] = o[g * tq:(g + 1) * tq, :]


def _prompt_attention(kind, q, k, v, caug, *, batch, seq, kvh, hd):
    t, qw = q.shape
    group = qw // (kvh * hd)
    tq = MOBA_BLOCK
    assert tq & (tq - 1) == 0
    nq = seq // tq
    rows = group * tq
    scale = hd ** -0.5
    width = 2 * hd if kind == "fox" else hd
    q_spec = pl.BlockSpec((tq, group * hd), lambda b, h, i: (b * nq + i, h))
    kv_spec = pl.BlockSpec((seq, hd), lambda b, h, i: (b, h))
    scratch = [pltpu.VMEM((nq, tq, width), bf16), pltpu.VMEM((nq, hd + BF16_ROWS, tq), bf16),
               pltpu.VMEM((rows, width), bf16), pltpu.VMEM((1, rows), f32), pltpu.VMEM((hd + BF16_ROWS, rows), f32)]
    in_specs = [q_spec, kv_spec, kv_spec]
    args = [q, k, v]
    if kind == "fox":
        in_specs.append(pl.BlockSpec((1, 1, seq, LANES), lambda b, h, i: (b, h, 0, 0)))
        args.append(caug)
    else:
        assert nq <= MAX_MOBA_BLOCKS
        sel_rows = -(-2 * nq // SUBLANES) * SUBLANES
        scratch += [pltpu.VMEM((LANES, hd), bf16), pltpu.VMEM((sel_rows, rows), f32)]
    return pl.pallas_call(
        functools.partial(_attn_kernel, kind=kind, group=group, tq=tq, nblk=nq, scale=scale),
        grid=(batch, kvh, nq),
        in_specs=in_specs,
        out_specs=q_spec,
        out_shape=jax.ShapeDtypeStruct((t, qw), f32),
        scratch_shapes=scratch,
        compiler_params=_params(("parallel", "parallel", "arbitrary"),
                                [((tq, group * hd), bf16), ((seq, hd), f32), ((seq, hd), f32),
                                 ((tq, group * hd), f32), ((seq, LANES), bf16)],
                                _nbytes((seq, 3 * hd), bf16) + 8 * _nbytes((tq, rows), f32)),
        name=kind + "_attention",
    )(*args)


def _block_diag_q(q, kvh):
    heads, hd = q.shape
    group = heads // kvh
    qt = jnp.concatenate([q] * kvh, axis=1)
    row_kv = lax.broadcasted_iota(jnp.int32, qt.shape, 0) // group
    col_kv = lax.broadcasted_iota(jnp.int32, qt.shape, 1) // hd
    return jnp.where(row_kv == col_kv, qt, jnp.zeros_like(qt))


def _own_kv_slab(x, kvh):
    heads, w = x.shape
    hd = w // kvh
    group = heads // kvh
    row_kv = lax.broadcasted_iota(jnp.int32, (heads, hd), 0) // group
    out = jnp.zeros((heads, hd), x.dtype)
    for g in range(kvh):
        out = out + jnp.where(row_kv == g, x[:, g * hd:(g + 1) * hd], 0.0)
    return out


def _page_rows(ref, kvh):
    page = ref.shape[1] // kvh
    return jnp.concatenate([ref[0, pl.ds(h, page, stride=kvh), :] for h in range(kvh)], axis=1)


def _decode_kernel(pt_ref, q_ref, *refs, kind, npages, kvh, ppb, scale):
    k_refs = refs[:npages]
    v_refs = refs[npages:2 * npages]
    if kind == "fox":
        lf_refs = refs[2 * npages:3 * npages]
        kn_ref, vn_ref, lfn_ref, o_ref = refs[3 * npages:]
    else:
        kn_ref, vn_ref, o_ref = refs[2 * npages:]
    page = k_refs[0].shape[1] // kvh
    qbd = _block_diag_q(q_ref[0], kvh)
    kn = kn_ref[0].astype(bf16).astype(f32)
    vn = vn_ref[0].astype(bf16).astype(f32)
    s_new = jnp.sum(qbd.astype(f32) * kn, axis=1, keepdims=True) * scale

    k_pages = [_page_rows(r, kvh) for r in k_refs]
    s = jnp.concatenate([lax.dot_general(qbd, kp.astype(bf16), _NT, preferred_element_type=f32) for kp in k_pages],
                        axis=1) * scale
    if kind == "fox":
        c_in = _prefix_lanes(jnp.concatenate([r[0] for r in lf_refs], axis=1))
        s = s - c_in
        s_new = s_new - (c_in[:, c_in.shape[1] - 1:] + lfn_ref[0])
    else:
        nblk = npages // ppb
        sums = [sum(jnp.sum(k_pages[n * ppb + i], axis=0, keepdims=True) for i in range(ppb)) for n in range(nblk)]
        kmean = jnp.concatenate(sums + [jnp.zeros((LANES - nblk, qbd.shape[1]), f32)], axis=0) / (ppb * page)
        gate = lax.dot_general(qbd, kmean.astype(bf16), _NT, preferred_element_type=f32)
        sel = _rank_select(gate, nblk, nblk, 1)
        s = s + jnp.concatenate([jnp.broadcast_to(sel[:, n:n + 1], (sel.shape[0], ppb * page)) for n in range(nblk)],
                                axis=1)
    m = jnp.maximum(jnp.max(s, axis=1, keepdims=True), s_new)
    p = jnp.exp(s - m)
    p_new = jnp.exp(s_new - m)
    l = jnp.sum(p, axis=1, keepdims=True) + p_new
    acc = p_new * vn
    for i in range(npages):
        acc = acc + jnp.dot(p[:, i * page:(i + 1) * page].astype(bf16), _page_rows(v_refs[i], kvh).astype(bf16),
                            preferred_element_type=f32)
    o_ref[0] = _own_kv_slab(acc / l, kvh)


def _decode_attention(kind, q, k_pool, v_pool, lf_pool, k_new, v_new, lf_new, page_table, *, kvh, hd):
    db, npages = page_table.shape
    heads = q.shape[1] // hd
    kvw = kvh * hd
    page = k_pool.shape[1]
    scale = hd ** -0.5
    ppb = MOBA_BLOCK // page
    assert npages % ppb == 0 and 2 * (npages // ppb) <= LANES
    q3 = q.reshape(db, heads, hd)

    k_flat = k_pool.reshape(k_pool.shape[0], page * kvh, hd)
    v_flat = v_pool.reshape(v_pool.shape[0], page * kvh, hd)

    def paged_kv(i):
        return pl.BlockSpec((1, page * kvh, hd), lambda b, pt: (pt[b * npages + i], 0, 0))

    per_seq = lambda shape: pl.BlockSpec((1,) + shape, lambda b, pt: (b, 0, 0))
    in_specs = [per_seq((heads, hd))] + [paged_kv(i) for i in range(npages)] * 2
    args = [q3] + [k_flat] * npages + [v_flat] * npages
    if kind == "fox":
        lf_t = jnp.swapaxes(lf_pool, 1, 2)
        in_specs += [pl.BlockSpec((1, heads, page), lambda b, pt, i=i: (pt[b * npages + i], 0, 0))
                     for i in range(npages)]
        args += [lf_t] * npages
        in_specs += [per_seq((1, kvw)), per_seq((1, kvw)), per_seq((heads, 1))]
        args += [k_new.reshape(db, 1, kvw), v_new.reshape(db, 1, kvw), lf_new.reshape(db, heads, 1)]
    else:
        in_specs += [per_seq((1, kvw)), per_seq((1, kvw))]
        args += [k_new.reshape(db, 1, kvw), v_new.reshape(db, 1, kvw)]
    out = pl.pallas_call(
        functools.partial(_decode_kernel, kind=kind, npages=npages, kvh=kvh, ppb=ppb, scale=scale),
        grid_spec=pltpu.PrefetchScalarGridSpec(
            num_scalar_prefetch=1,
            grid=(db,),
            in_specs=in_specs,
            out_specs=pl.BlockSpec((1, heads, hd), lambda b, pt: (b, 0, 0)),
        ),
        out_shape=jax.ShapeDtypeStruct((db, heads, hd), f32),
        compiler_params=_params(("arbitrary",), [((page, kvw), f32)] * (2 * npages),
                                6 * npages * _nbytes((page, kvw), bf16)),
        name=kind + "_decode",
    )(page_table.reshape(-1), *args)
    return out.reshape(db, heads * hd)


def _rope_tables(pos, hd):
    half = hd // 2
    inv_freq = ROPE_THETA ** (-jnp.arange(half, dtype=f32) / half)
    ang = pos.astype(f32)[:, None] * inv_freq[None, :]
    cos, sin = jnp.cos(ang), jnp.sin(ang)
    return jnp.concatenate([cos, cos], axis=-1), jnp.concatenate([-sin, sin], axis=-1)


def _half_ffn(h, hn, g_post, w_gate, w_up, w_down, g_next):
    a = _gate_up(hn, w_gate, w_up)
    y = _matmul(a, w_down, MM_COLS)
    return _resnorm(y, h, g_post, 0.5, g_next)


def kernel(x_prompt, x_sample, cache_fox_k, cache_fox_v, cache_fox_logf, cache_moba_k, cache_moba_v, page_table,
           g_pre_ffn1, g_post_ffn1, w1_gate, w1_up, w1_down, g_pre_mix, w_in, b_forget, g_out_fox, g_out_moba,
           w_out, g_post_mix, g_pre_ffn2, g_post_ffn2, w2_gate, w2_up, w2_down):
    depth = w_in.shape[0]
    assert depth == 1, "one layer: the sample group reads the caches the prompt group does not write"
    batch, seq, d = x_prompt.shape
    db, dseq, _ = x_sample.shape
    assert dseq == 1
    kvh, hd = cache_fox_k.shape[3], cache_fox_k.shape[4]
    assert hd == LANES
    kvw = kvh * hd
    fh = b_forget.shape[-1]
    fw = fh * hd
    mw = w_in.shape[-1] - fw - 4 * kvw - fh
    page = cache_fox_k.shape[2]
    past_len = page_table.shape[1] * page
    assert seq % MOBA_BLOCK == 0 and past_len % MOBA_BLOCK == 0

    def layer0(a):
        return a.reshape(a.shape[1:])

    def trunk(x, pos, attend):
        hn = _prenorm(x, layer0(g_pre_ffn1))
        h, hn = _half_ffn(x, hn, layer0(g_post_ffn1), layer0(w1_gate), layer0(w1_up), layer0(w1_down),
                          layer0(g_pre_mix))
        p = _matmul(hn, layer0(w_in), MM_COLS_WIDE)
        cosf, sinf = _rope_tables(pos, hd)
        qf, kf, vf, qm, km, vm, lf = _postproj(p, cosf, sinf, layer0(b_forget), fw=fw, kvw=kvw, mw=mw, fh=fh, hd=hd)
        o_f, o_m = attend(qf, kf, vf, qm, km, vm, lf)
        z = _groupnorm(o_f, o_m, layer0(g_out_fox), layer0(g_out_moba))
        y = _matmul(z, layer0(w_out), MM_COLS_WIDE)
        h, hn = _resnorm(y, h, layer0(g_post_mix), 1.0, layer0(g_pre_ffn2))
        out, _ = _half_ffn(h, hn, layer0(g_post_ffn2), layer0(w2_gate), layer0(w2_up), layer0(w2_down), None)
        return out, (kf, vf, lf, km, vm)

    def attend_prompt(qf, kf, vf, qm, km, vm, lf):
        caug = _cumsum_logf(lf.reshape(batch, seq, fh), kvh=kvh, scale=hd ** -0.5)
        o_f = _prompt_attention("fox", qf, kf, vf, caug, batch=batch, seq=seq, kvh=kvh, hd=hd)
        o_m = _prompt_attention("moba", qm, km, vm, None, batch=batch, seq=seq, kvh=kvh, hd=hd)
        return o_f, o_m

    def attend_sample(qf, kf, vf, qm, km, vm, lf):
        o_f = _decode_attention("fox", qf, layer0(cache_fox_k), layer0(cache_fox_v), layer0(cache_fox_logf),
                                kf, vf, lf, page_table, kvh=kvh, hd=hd)
        o_m = _decode_attention("moba", qm, layer0(cache_moba_k), layer0(cache_moba_v), None, km, vm, None,
                                page_table, kvh=kvh, hd=hd)
        return o_f, o_m

    prompt_pos = jnp.tile(jnp.arange(seq, dtype=jnp.int32), batch)
    sample_pos = jnp.full((db,), past_len, jnp.int32)
    yp, (pfk, pfv, pfl, pmk, pmv) = trunk(x_prompt.reshape(batch * seq, d), prompt_pos, attend_prompt)
    ys, (sfk, sfv, sfl, smk, smv) = trunk(x_sample.reshape(db, d), sample_pos, attend_sample)

    def rows(a, b, s):
        return a.reshape(depth, b, s, kvh, hd)

    return (yp.reshape(batch, seq, d), ys.reshape(db, dseq, d),
            rows(pfk, batch, seq), rows(pfv, batch, seq), pfl.reshape(depth, batch, seq, fh),
            rows(pmk, batch, seq), rows(pmv, batch, seq),
            rows(sfk, db, dseq), rows(sfv, db, dseq), sfl.reshape(depth, db, dseq, fh),
            rows(smk, db, dseq), rows(smv, db, dseq))
```

```python
import functools
import math

import jax
import jax.numpy as jnp
from jax import lax
from jax.experimental import pallas as pl
from jax.experimental.pallas import tpu as pltpu

f32 = jnp.float32
bf16 = jnp.bfloat16

MOBA_BLOCK = 256
MOBA_TOPK = 3
ROPE_THETA = 10000.0
RMS_EPS = 1e-6
LANES = 128
SUBLANES = 8
BF16_ROWS = 16
ROW_TILE = 256
MM_ROWS = 1664
MM_ROWS_WIDE_K = 1040
PROJ_ROWS = 832
MM_COLS = 256
MM_COLS_WIDE = 512
VMEM_CAP = 60 * 1024 * 1024

_NT = (((1,), (1,)), ((), ()))


def _nbytes(shape, dtype):
    n = 1
    for s in shape:
        n *= s
    return n * jnp.dtype(dtype).itemsize


def _params(sem, pipelined, resident=0):
    est = 2 * sum(_nbytes(s, d) for s, d in pipelined) + resident
    limit = min(VMEM_CAP, max(32 * 1024 * 1024, est + est // 4))
    return pltpu.CompilerParams(dimension_semantics=sem, vmem_limit_bytes=int(limit))


def _divisor_tile(t, cap, mult):
    best = None
    for c in range(mult, min(t, cap) + 1, mult):
        if t % c == 0:
            best = c
    assert best is not None, (t, cap, mult)
    return best


def _rmsnorm(x, g):
    ms = jnp.mean(x * x, axis=-1, keepdims=True)
    return x * lax.rsqrt(ms + RMS_EPS) * g


def _part_tiles(parts, tr):
    spans, start = [], 0
    for a in parts:
        assert a.shape[0] % tr == 0
        spans.append((start, a.shape[0] // tr))
        start += a.shape[0] // tr
    return spans, start


def _part_spec(span, tr, width):
    first, count = span
    return pl.BlockSpec((tr, width), lambda i: (jnp.clip(i - first, 0, count - 1), 0))


def _read_parts(refs, spans):
    i = pl.program_id(0)
    x = refs[0][...]
    for r, (first, _) in zip(refs[1:], spans[1:]):
        x = jnp.where(i >= first, r[...], x)
    return x


def _write_parts(refs, spans, x):
    i = pl.program_id(0)
    if len(refs) == 1:
        refs[0][...] = x.astype(refs[0].dtype)
        return
    for r, (first, count) in zip(refs, spans):
        @pl.when((i >= first) & (i < first + count))
        def _(r=r):
            r[...] = x.astype(r.dtype)


def _row_tile(parts):
    return math.gcd(ROW_TILE, *[a.shape[0] for a in parts])


def _prenorm_kernel(*refs, spans):
    n = len(spans)
    g_ref, o_ref = refs[n], refs[n + 1]
    o_ref[...] = _rmsnorm(_read_parts(refs[:n], spans), g_ref[...]).astype(o_ref.dtype)


def _prenorm(x_parts, g):
    d = x_parts[0].shape[1]
    tr = _row_tile(x_parts)
    spans, tiles = _part_tiles(x_parts, tr)
    return pl.pallas_call(
        functools.partial(_prenorm_kernel, spans=spans),
        grid=(tiles,),
        in_specs=[_part_spec(s, tr, d) for s in spans] + [pl.BlockSpec((1, d), lambda i: (0, 0))],
        out_specs=pl.BlockSpec((tr, d), lambda i: (i, 0)),
        out_shape=jax.ShapeDtypeStruct((tiles * tr, d), bf16),
        compiler_params=_params(("arbitrary",), [((tr, d), f32)] * len(spans) + [((tr, d), bf16)],
                                2 * _nbytes((tr, d), f32)),
        name="prenorm",
    )(*x_parts, g.reshape(1, d))


def _resnorm_kernel(*refs, res_spans, out_spans, weight, has_next):
    y_ref = refs[0]
    nres = len(res_spans)
    res_refs = refs[1:1 + nres]
    gp_ref = refs[1 + nres]
    rest = refs[2 + nres:]
    y = _rmsnorm(y_ref[...], gp_ref[...])
    if weight != 1.0:
        y = weight * y
    h = _read_parts(res_refs, res_spans) + y
    if has_next:
        gn_ref, h_ref, hn_ref = rest
        h_ref[...] = h
        hn_ref[...] = _rmsnorm(h, gn_ref[...]).astype(hn_ref.dtype)
    else:
        _write_parts(rest, out_spans, h)


def _resnorm(y, res_parts, g_post, weight, g_next=None, out_rows=None):
    t, d = y.shape
    tr = _row_tile(res_parts)
    res_spans, tiles = _part_tiles(res_parts, tr)
    assert tiles * tr == t
    row = pl.BlockSpec((tr, d), lambda i: (i, 0))
    vec = pl.BlockSpec((1, d), lambda i: (0, 0))
    has_next = g_next is not None
    in_specs = [row] + [_part_spec(s, tr, d) for s in res_spans] + [vec] + ([vec] if has_next else [])
    args = [y, *res_parts, g_post.reshape(1, d)] + ([g_next.reshape(1, d)] if has_next else [])
    if has_next:
        out_spans = None
        out_specs = [row, row]
        out_shape = [jax.ShapeDtypeStruct((t, d), f32), jax.ShapeDtypeStruct((t, d), bf16)]
    else:
        out_rows = out_rows or [t]
        assert all(r % tr == 0 for r in out_rows) and sum(out_rows) == t
        out_spans, _ = _part_tiles([jax.ShapeDtypeStruct((r, d), f32) for r in out_rows], tr)
        out_specs = [_part_spec(s, tr, d) for s in out_spans]
        out_shape = [jax.ShapeDtypeStruct((r, d), f32) for r in out_rows]
    return pl.pallas_call(
        functools.partial(_resnorm_kernel, res_spans=res_spans, out_spans=out_spans, weight=weight, has_next=has_next),
        grid=(tiles,),
        in_specs=in_specs,
        out_specs=out_specs,
        out_shape=out_shape,
        compiler_params=_params(("arbitrary",), [((tr, d), f32)] * 5, 3 * _nbytes((tr, d), f32)),
        name="resnorm",
    )(*args)


def _groupnorm_kernel(*refs, spans):
    n = len(spans)
    of_refs, om_refs = refs[:n], refs[n:2 * n]
    gf_ref, gm_ref, z_ref = refs[2 * n:]
    fw = of_refs[0].shape[1]
    z_ref[:, :fw] = _rmsnorm(_read_parts(of_refs, spans), gf_ref[...]).astype(z_ref.dtype)
    z_ref[:, fw:] = _rmsnorm(_read_parts(om_refs, spans), gm_ref[...]).astype(z_ref.dtype)


def _groupnorm(of_parts, om_parts, g_f, g_m):
    fw, mw = of_parts[0].shape[1], om_parts[0].shape[1]
    tr = _row_tile(of_parts)
    spans, tiles = _part_tiles(of_parts, tr)
    return pl.pallas_call(
        functools.partial(_groupnorm_kernel, spans=spans),
        grid=(tiles,),
        in_specs=[_part_spec(s, tr, fw) for s in spans] + [_part_spec(s, tr, mw) for s in spans]
        + [pl.BlockSpec((1, fw), lambda i: (0, 0)), pl.BlockSpec((1, mw), lambda i: (0, 0))],
        out_specs=pl.BlockSpec((tr, fw + mw), lambda i: (i, 0)),
        out_shape=jax.ShapeDtypeStruct((tiles * tr, fw + mw), bf16),
        compiler_params=_params(("arbitrary",), [((tr, fw + mw), f32)] * len(spans) + [((tr, fw + mw), bf16)],
                                2 * _nbytes((tr, fw + mw), f32)),
        name="groupnorm",
    )(*of_parts, *om_parts, g_f.reshape(1, fw), g_m.reshape(1, mw))


def _prefix_lanes(x):
    lane = lax.broadcasted_iota(jnp.int32, x.shape, 1)
    sh = 1
    while sh < x.shape[1]:
        x = x + jnp.where(lane >= sh, pltpu.roll(x, sh, 1), 0.0)
        sh *= 2
    return x


def _cumsum_kernel(lf_ref, c_ref, *, group, inv_scale):
    c = _prefix_lanes(lf_ref[...].T)
    bt = (c * (-inv_scale)).T
    pad = jnp.zeros((bt.shape[0], LANES - 3 * group), f32)
    for kv in range(c_ref.shape[1]):
        b = bt[:, kv * group:(kv + 1) * group]
        hi = b.astype(bf16).astype(f32)
        mid = (b - hi).astype(bf16).astype(f32)
        lo = b - hi - mid
        c_ref[0, kv] = jnp.concatenate([hi, mid, lo, pad], axis=1).astype(bf16)


def _cumsum_logf(logf, *, batch, seq, kvh, scale):
    fh = logf.shape[1]
    g = fh // kvh
    assert 3 * g <= LANES
    return pl.pallas_call(
        functools.partial(_cumsum_kernel, group=g, inv_scale=1.0 / scale),
        grid=(batch,),
        in_specs=[pl.BlockSpec((seq, fh), lambda i: (i, 0))],
        out_specs=pl.BlockSpec((1, kvh, seq, LANES), lambda i: (i, 0, 0, 0)),
        out_shape=jax.ShapeDtypeStruct((batch, kvh, seq, LANES), bf16),
        compiler_params=_params(("arbitrary",), [((seq, LANES), f32), ((kvh * seq, LANES), bf16)],
                                8 * _nbytes((seq, LANES), f32)),
        name="cumsum_logf",
    )(logf)


def _gate_up_kernel(x_ref, wg_ref, wu_ref, o_ref):
    x = x_ref[...]
    g = jnp.dot(x, wg_ref[...].astype(bf16), preferred_element_type=f32)
    u = jnp.dot(x, wu_ref[...].astype(bf16), preferred_element_type=f32)
    o_ref[...] = (jax.nn.silu(g) * u).astype(o_ref.dtype)


def _gate_up(xn, w_gate, w_up):
    t, d = xn.shape
    ff = w_gate.shape[1]
    tm = _divisor_tile(t, MM_ROWS, BF16_ROWS)
    tn = MM_COLS
    return pl.pallas_call(
        _gate_up_kernel,
        grid=(t // tm, pl.cdiv(ff, tn)),
        in_specs=[pl.BlockSpec((tm, d), lambda i, j: (i, 0), pipeline_mode=pl.Buffered(1)),
                  pl.BlockSpec((d, tn), lambda i, j: (0, j)),
                  pl.BlockSpec((d, tn), lambda i, j: (0, j))],
        out_specs=pl.BlockSpec((tm, tn), lambda i, j: (i, j)),
        out_shape=jax.ShapeDtypeStruct((t, ff), bf16),
        compiler_params=_params(("arbitrary", "arbitrary"),
                                [((d, tn), f32), ((d, tn), f32), ((tm, tn), bf16)],
                                _nbytes((tm, d), bf16) + 2 * _nbytes((d, tn), bf16) + 4 * _nbytes((tm, tn), f32)),
        name="gate_up",
    )(xn, w_gate, w_up)


def _matmul_kernel(x_ref, w_ref, o_ref):
    o_ref[...] = jnp.dot(x_ref[...], w_ref[...].astype(bf16), preferred_element_type=f32)


def _matmul(x, w, tn):
    t, k = x.shape
    n = w.shape[1]
    tm = _divisor_tile(t, MM_ROWS if k * tn <= 4096 * MM_COLS_WIDE else MM_ROWS_WIDE_K, BF16_ROWS)
    return pl.pallas_call(
        _matmul_kernel,
        grid=(t // tm, pl.cdiv(n, tn)),
        in_specs=[pl.BlockSpec((tm, k), lambda i, j: (i, 0), pipeline_mode=pl.Buffered(1)),
                  pl.BlockSpec((k, tn), lambda i, j: (0, j))],
        out_specs=pl.BlockSpec((tm, tn), lambda i, j: (i, j)),
        out_shape=jax.ShapeDtypeStruct((t, n), f32),
        compiler_params=_params(("arbitrary", "arbitrary"),
                                [((k, tn), f32), ((tm, tn), f32)],
                                _nbytes((tm, k), bf16) + _nbytes((k, tn), bf16) + 2 * _nbytes((tm, tn), f32)),
        name="matmul",
    )(x, w)


def _proj_kernel(x_ref, wt_ref, cos_ref, sin_ref, b_ref, *out_refs, bounds, hd, fh):
    j = pl.program_id(1)
    acc = lax.dot_general(x_ref[...], wt_ref[...].astype(bf16), _NT, preferred_element_type=f32)
    qf_ref, kf_ref, vf_ref, qm_ref, km_ref, vm_ref, lf_ref = out_refs

    def rope(ref):
        cosf, sinf = cos_ref[...], sin_ref[...]
        for h in range(acc.shape[1] // hd):
            x = acc[:, h * hd:(h + 1) * hd]
            ref[:, h * hd:(h + 1) * hd] = (x * cosf + pltpu.roll(x, hd // 2, 1) * sinf).astype(ref.dtype)

    def plain(ref):
        ref[...] = acc.astype(ref.dtype)

    def forget(ref):
        ref[...] = jax.nn.log_sigmoid(acc[:, :fh] + b_ref[...])

    actions = [plain, plain, plain, rope, rope, plain, forget]
    for ref, act, (lo, hi) in zip(out_refs, actions, bounds):
        @pl.when((j >= lo) & (j < hi))
        def _(ref=ref, act=act):
            act(ref)


def _proj(xn, w_in_t, cosf, sinf, b_forget, *, fw, kvw, mw, fh, hd):
    t, d = xn.shape
    tn = math.gcd(MM_COLS_WIDE, fw, kvw, mw)
    tm = _divisor_tile(t, PROJ_ROWS, BF16_ROWS)
    widths = [fw, kvw, kvw, mw, kvw, kvw]
    bounds, lo = [], 0
    for w in widths:
        bounds.append((lo, lo + w // tn))
        lo += w // tn
    bounds.append((lo, lo + 1))
    assert lo * tn + fh == w_in_t.shape[0] and fh <= tn

    def out_spec(width, lo, hi):
        return pl.BlockSpec((tm, width), lambda i, j: (i, jnp.clip(j - lo, 0, hi - lo - 1)))

    out_specs = [out_spec(tn, lo, hi) for lo, hi in bounds[:-1]] + [pl.BlockSpec((tm, fh), lambda i, j: (i, 0))]
    dtypes = [bf16, f32, f32, bf16, f32, f32, f32]
    out_shape = [jax.ShapeDtypeStruct((t, w), dt) for w, dt in zip(widths + [fh], dtypes)]
    return pl.pallas_call(
        functools.partial(_proj_kernel, bounds=bounds, hd=hd, fh=fh),
        grid=(t // tm, bounds[-1][1]),
        in_specs=[pl.BlockSpec((tm, d), lambda i, j: (i, 0), pipeline_mode=pl.Buffered(1)),
                  pl.BlockSpec((tn, d), lambda i, j: (j, 0)),
                  pl.BlockSpec((tm, hd), lambda i, j: (i, 0)),
                  pl.BlockSpec((tm, hd), lambda i, j: (i, 0)),
                  pl.BlockSpec((1, fh), lambda i, j: (0, 0))],
        out_specs=out_specs,
        out_shape=out_shape,
        compiler_params=_params(("arbitrary", "arbitrary"),
                                [((tn, d), f32)] + [((tm, tn), f32)] * 7 + [((tm, hd), f32)] * 2,
                                _nbytes((tm, d), bf16) + _nbytes((tn, d), bf16) + 3 * _nbytes((tm, tn), f32)),
        name="proj",
    )(xn, w_in_t, cosf, sinf, b_forget.reshape(1, fh))


MAX_MOBA_BLOCKS = LANES // 2


def _rank_select(gate, n_valid, nblk, axis):
    idx = lax.broadcasted_iota(jnp.int32, gate.shape, axis)
    valid = idx < n_valid
    g = jnp.where(valid, gate, -jnp.inf)
    rank = jnp.zeros(gate.shape, f32)
    size = gate.shape[axis]
    assert size >= 2 * nblk
    for d in range(1, nblk):
        lower = pltpu.roll(g, d, axis)
        upper = pltpu.roll(g, size - d, axis)
        rank = rank + (lower >= g).astype(f32) + (upper > g).astype(f32)
    return jnp.where(valid & (rank < MOBA_TOPK), 0.0, -jnp.inf)


def _attn_kernel(*refs, kind, group, tq, nblk, scale):
    if kind == "fox":
        q_ref, k_ref, v_ref, c_ref, o_ref, kb_ref, vt_ref, qa_ref, m_ref, acc_ref = refs
    else:
        q_ref, k_ref, v_ref, o_ref, kb_ref, vt_ref, qa_ref, m_ref, acc_ref, km_ref, sel_ref = refs
    qi = pl.program_id(2)
    hd = k_ref.shape[-1]
    rows = group * tq

    @pl.when(qi == 0)
    def _():
        for n in range(nblk):
            kn = k_ref[n * tq:(n + 1) * tq, :]
            kb_ref[n, :, :hd] = kn.astype(bf16)
            if kind == "fox":
                kb_ref[n, :, hd:] = c_ref[0, 0, n * tq:(n + 1) * tq, :]
            vt_ref[n, :hd, :] = v_ref[n * tq:(n + 1) * tq, :].T.astype(bf16)
            vt_ref[n, hd:, :] = jnp.ones((BF16_ROWS, tq), bf16)
        if kind == "moba":
            means = [jnp.mean(k_ref[n * tq:(n + 1) * tq, :], axis=0, keepdims=True) for n in range(nblk)]
            km_ref[...] = jnp.concatenate(means + [jnp.zeros((LANES - nblk, hd), f32)], axis=0).astype(bf16)

    q = q_ref[...]
    for g in range(group):
        qa_ref[g * tq:(g + 1) * tq, :hd] = q[:, g * hd:(g + 1) * hd]
    lane_q = lax.broadcasted_iota(jnp.int32, (tq, rows), 1)
    if kind == "fox":
        lane = lax.broadcasted_iota(jnp.int32, (rows, hd), 1)
        head = lax.broadcasted_iota(jnp.int32, (rows, hd), 0) // tq
        pick = (lane == head) | (lane == head + group) | (lane == head + 2 * group)
        qa_ref[:, hd:] = jnp.where(pick, 1.0, 0.0).astype(bf16)
    else:
        gate = lax.dot_general(km_ref[...], qa_ref[...], _NT, preferred_element_type=f32)
        sel_ref[...] = _rank_select(gate[:sel_ref.shape[0]], qi, nblk, 0)

    m_ref[...] = jnp.full(m_ref.shape, -jnp.inf, f32)
    acc_ref[...] = jnp.zeros(acc_ref.shape, f32)

    def step(j, diagonal):
        s = lax.dot_general(kb_ref[j], qa_ref[...], _NT, preferred_element_type=f32) * scale
        if diagonal:
            key = lax.broadcasted_iota(jnp.int32, s.shape, 0)
            s = jnp.where(key <= (lane_q & (tq - 1)), s, -jnp.inf)
        elif kind == "moba":
            s = s + sel_ref[pl.ds(j, 1), :]
        m_prev = m_ref[...]
        m_new = jnp.maximum(m_prev, jnp.max(s, axis=0, keepdims=True))
        alpha = jnp.exp(m_prev - m_new)
        p = jnp.exp(s - m_new).astype(bf16)
        acc_ref[...] = alpha * acc_ref[...] + jnp.dot(vt_ref[j], p, preferred_element_type=f32)
        m_ref[...] = m_new

    step(qi, True)

    def body(i, carry):
        step(2 * i, False)
        step(2 * i + 1, False)
        return carry

    lax.fori_loop(0, lax.shift_right_logical(qi, 1), body, 0)

    @pl.when((qi & 1) == 1)
    def _():
        step(qi - 1, False)

    acc = acc_ref[...]
    o = (acc[:hd] / acc[hd:hd + 1]).T
    for g in range(group):
        o_ref[:, g * hd:(g + 1) * hd] = o[g * tq:(g + 1) * tq, :]


def _prompt_attention(kind, q, k, v, caug, *, batch, seq, kvh, hd):
    qw = q.shape[1]
    group = qw // (kvh * hd)
    tq = MOBA_BLOCK
    assert tq & (tq - 1) == 0
    nq = seq // tq
    rows = group * tq
    scale = hd ** -0.5
    width = 2 * hd if kind == "fox" else hd
    q_spec = pl.BlockSpec((tq, group * hd), lambda b, h, i: (b * nq + i, h))
    kv_spec = pl.BlockSpec((seq, hd), lambda b, h, i: (b, h))
    scratch = [pltpu.VMEM((nq, tq, width), bf16), pltpu.VMEM((nq, hd + BF16_ROWS, tq), bf16),
               pltpu.VMEM((rows, width), bf16), pltpu.VMEM((1, rows), f32), pltpu.VMEM((hd + BF16_ROWS, rows), f32)]
    in_specs = [q_spec, kv_spec, kv_spec]
    args = [q, k, v]
    if kind == "fox":
        in_specs.append(pl.BlockSpec((1, 1, seq, LANES), lambda b, h, i: (b, h, 0, 0)))
        args.append(caug)
    else:
        assert nq <= MAX_MOBA_BLOCKS
        sel_rows = -(-2 * nq // SUBLANES) * SUBLANES
        scratch += [pltpu.VMEM((LANES, hd), bf16), pltpu.VMEM((sel_rows, rows), f32)]
    return pl.pallas_call(
        functools.partial(_attn_kernel, kind=kind, group=group, tq=tq, nblk=nq, scale=scale),
        grid=(batch, kvh, nq),
        in_specs=in_specs,
        out_specs=q_spec,
        out_shape=jax.ShapeDtypeStruct((batch * seq, qw), f32),
        scratch_shapes=scratch,
        compiler_params=_params(("arbitrary", "arbitrary", "arbitrary"),
                                [((tq, group * hd), bf16), ((seq, hd), f32), ((seq, hd), f32),
                                 ((tq, group * hd), f32), ((seq, LANES), bf16)],
                                _nbytes((seq, 3 * hd), bf16) + 8 * _nbytes((tq, rows), f32)),
        name=kind + "_attention",
    )(*args)


def _block_diag_q(q, kvh):
    heads, hd = q.shape
    group = heads // kvh
    qt = jnp.concatenate([q] * kvh, axis=1)
    row_kv = lax.broadcasted_iota(jnp.int32, qt.shape, 0) // group
    col_kv = lax.broadcasted_iota(jnp.int32, qt.shape, 1) // hd
    return jnp.where(row_kv == col_kv, qt, jnp.zeros_like(qt))


def _own_kv_slab(x, kvh):
    heads, w = x.shape
    hd = w // kvh
    group = heads // kvh
    row_kv = lax.broadcasted_iota(jnp.int32, (heads, hd), 0) // group
    out = jnp.zeros((heads, hd), x.dtype)
    for g in range(kvh):
        out = out + jnp.where(row_kv == g, x[:, g * hd:(g + 1) * hd], 0.0)
    return out


def _page_rows(ref, kvh):
    page = ref.shape[1] // kvh
    return jnp.concatenate([ref[0, pl.ds(h, page, stride=kvh), :] for h in range(kvh)], axis=1)


def _decode_kernel(pt_ref, q_ref, *refs, kind, npages, kvh, ppb, scale):
    k_refs = refs[:npages]
    v_refs = refs[npages:2 * npages]
    if kind == "fox":
        lf_refs = refs[2 * npages:3 * npages]
        kn_ref, vn_ref, lfn_ref, o_ref = refs[3 * npages:]
    else:
        kn_ref, vn_ref, o_ref = refs[2 * npages:]
    page = k_refs[0].shape[1] // kvh
    qbd = _block_diag_q(q_ref[0], kvh)
    kn = kn_ref[0].astype(bf16).astype(f32)
    vn = vn_ref[0].astype(bf16).astype(f32)
    s_new = jnp.sum(qbd.astype(f32) * kn, axis=1, keepdims=True) * scale

    k_pages = [_page_rows(r, kvh) for r in k_refs]
    s = jnp.concatenate([lax.dot_general(qbd, kp.astype(bf16), _NT, preferred_element_type=f32) for kp in k_pages],
                        axis=1) * scale
    if kind == "fox":
        c_in = _prefix_lanes(jnp.concatenate([r[0] for r in lf_refs], axis=1))
        s = s - c_in
        s_new = s_new - (c_in[:, c_in.shape[1] - 1:] + lfn_ref[0])
    else:
        nblk = npages // ppb
        sums = [sum(jnp.sum(k_pages[n * ppb + i], axis=0, keepdims=True) for i in range(ppb)) for n in range(nblk)]
        kmean = jnp.concatenate(sums + [jnp.zeros((LANES - nblk, qbd.shape[1]), f32)], axis=0) / (ppb * page)
        gate = lax.dot_general(qbd, kmean.astype(bf16), _NT, preferred_element_type=f32)
        sel = _rank_select(gate, nblk, nblk, 1)
        s = s + jnp.concatenate([jnp.broadcast_to(sel[:, n:n + 1], (sel.shape[0], ppb * page)) for n in range(nblk)],
                                axis=1)
    m = jnp.maximum(jnp.max(s, axis=1, keepdims=True), s_new)
    p = jnp.exp(s - m)
    p_new = jnp.exp(s_new - m)
    l = jnp.sum(p, axis=1, keepdims=True) + p_new
    acc = p_new * vn
    for i in range(npages):
        acc = acc + jnp.dot(p[:, i * page:(i + 1) * page].astype(bf16), _page_rows(v_refs[i], kvh).astype(bf16),
                            preferred_element_type=f32)
    o_ref[0] = _own_kv_slab(acc / l, kvh)


def _decode_attention(kind, q, k_pool, v_pool, lf_pool, k_new, v_new, lf_new, page_table, *, kvh, hd):
    db, npages = page_table.shape
    heads = q.shape[1] // hd
    kvw = kvh * hd
    page = k_pool.shape[1]
    scale = hd ** -0.5
    ppb = MOBA_BLOCK // page
    assert npages % ppb == 0 and 2 * (npages // ppb) <= LANES
    q3 = q.reshape(db, heads, hd)
    k_flat = k_pool.reshape(k_pool.shape[0], page * kvh, hd)
    v_flat = v_pool.reshape(v_pool.shape[0], page * kvh, hd)

    def paged_kv(i):
        return pl.BlockSpec((1, page * kvh, hd), lambda b, pt: (pt[b * npages + i], 0, 0))

    per_seq = lambda shape: pl.BlockSpec((1,) + shape, lambda b, pt: (b, 0, 0))
    in_specs = [per_seq((heads, hd))] + [paged_kv(i) for i in range(npages)] * 2
    args = [q3] + [k_flat] * npages + [v_flat] * npages
    if kind == "fox":
        lf_t = jnp.swapaxes(lf_pool, 1, 2)
        in_specs += [pl.BlockSpec((1, heads, page), lambda b, pt, i=i: (pt[b * npages + i], 0, 0))
                     for i in range(npages)]
        args += [lf_t] * npages
        in_specs += [per_seq((1, kvw)), per_seq((1, kvw)), per_seq((heads, 1))]
        args += [k_new.reshape(db, 1, kvw), v_new.reshape(db, 1, kvw), lf_new.reshape(db, heads, 1)]
    else:
        in_specs += [per_seq((1, kvw)), per_seq((1, kvw))]
        args += [k_new.reshape(db, 1, kvw), v_new.reshape(db, 1, kvw)]
    out = pl.pallas_call(
        functools.partial(_decode_kernel, kind=kind, npages=npages, kvh=kvh, ppb=ppb, scale=scale),
        grid_spec=pltpu.PrefetchScalarGridSpec(
            num_scalar_prefetch=1,
            grid=(db,),
            in_specs=in_specs,
            out_specs=pl.BlockSpec((1, heads, hd), lambda b, pt: (b, 0, 0)),
        ),
        out_shape=jax.ShapeDtypeStruct((db, heads, hd), f32),
        compiler_params=_params(("arbitrary",), [((page, kvw), f32)] * (2 * npages),
                                6 * npages * _nbytes((page, kvw), bf16)),
        name=kind + "_decode",
    )(page_table.reshape(-1), *args)
    return out.reshape(db, heads * hd)


def _rope_tables(pos, hd):
    half = hd // 2
    inv_freq = ROPE_THETA ** (-jnp.arange(half, dtype=f32) / half)
    ang = pos.astype(f32)[:, None] * inv_freq[None, :]
    cos, sin = jnp.cos(ang), jnp.sin(ang)
    return jnp.concatenate([cos, cos], axis=-1), jnp.concatenate([-sin, sin], axis=-1)


def kernel(x_prompt, x_sample, cache_fox_k, cache_fox_v, cache_fox_logf, cache_moba_k, cache_moba_v, page_table,
           g_pre_ffn1, g_post_ffn1, w1_gate, w1_up, w1_down, g_pre_mix, w_in, b_forget, g_out_fox, g_out_moba,
           w_out, g_post_mix, g_pre_ffn2, g_post_ffn2, w2_gate, w2_up, w2_down):
    depth = w_in.shape[0]
    assert depth == 1, "one layer: the sample group reads the caches the prompt group does not write"
    batch, seq, d = x_prompt.shape
    db, dseq, _ = x_sample.shape
    assert dseq == 1
    kvh, hd = cache_fox_k.shape[3], cache_fox_k.shape[4]
    assert hd == LANES
    kvw = kvh * hd
    fh = b_forget.shape[-1]
    fw = fh * hd
    mw = w_in.shape[-1] - fw - 4 * kvw - fh
    page = cache_fox_k.shape[2]
    past_len = page_table.shape[1] * page
    assert seq % MOBA_BLOCK == 0 and past_len % MOBA_BLOCK == 0
    tp = batch * seq

    def layer0(a):
        return a.reshape(a.shape[1:])

    def half_ffn(res_parts, hn, g_post, w_gate, w_up, w_down, **kw):
        a = _gate_up(hn, layer0(w_gate), layer0(w_up))
        y = _matmul(a, layer0(w_down), MM_COLS)
        return _resnorm(y, res_parts, layer0(g_post), 0.5, **kw)

    x_parts = [x_prompt.reshape(tp, d), x_sample.reshape(db, d)]
    hn = _prenorm(x_parts, layer0(g_pre_ffn1))
    h, hn = half_ffn(x_parts, hn, g_post_ffn1, w1_gate, w1_up, w1_down, g_next=layer0(g_pre_mix))

    pos = jnp.concatenate([jnp.tile(jnp.arange(seq, dtype=jnp.int32), batch), jnp.full((db,), past_len, jnp.int32)])
    cosf, sinf = _rope_tables(pos, hd)
    w_in_t = jnp.swapaxes(layer0(w_in), 0, 1)
    qf, kf, vf, qm, km, vm, lf = _proj(hn, w_in_t, cosf, sinf, layer0(b_forget), fw=fw, kvw=kvw, mw=mw, fh=fh, hd=hd)

    scale = hd ** -0.5
    caug = _cumsum_logf(lf, batch=batch, seq=seq, kvh=kvh, scale=scale)
    of_p = _prompt_attention("fox", qf, kf, vf, caug, batch=batch, seq=seq, kvh=kvh, hd=hd)
    om_p = _prompt_attention("moba", qm, km, vm, None, batch=batch, seq=seq, kvh=kvh, hd=hd)
    of_s = _decode_attention("fox", qf[tp:], layer0(cache_fox_k), layer0(cache_fox_v), layer0(cache_fox_logf),
                             kf[tp:], vf[tp:], lf[tp:], page_table, kvh=kvh, hd=hd)
    om_s = _decode_attention("moba", qm[tp:], layer0(cache_moba_k), layer0(cache_moba_v), None,
                             km[tp:], vm[tp:], None, page_table, kvh=kvh, hd=hd)

    z = _groupnorm([of_p, of_s], [om_p, om_s], layer0(g_out_fox), layer0(g_out_moba))
    y = _matmul(z, layer0(w_out), MM_COLS_WIDE)
    h, hn = _resnorm(y, [h], layer0(g_post_mix), 1.0, g_next=layer0(g_pre_ffn2))
    yp, ys = half_ffn([h], hn, g_post_ffn2, w2_gate, w2_up, w2_down, out_rows=[tp, db])

    def rows(a, lo, b, s):
        return a[lo:lo + b * s].reshape(depth, b, s, kvh, hd)

    return (yp.reshape(batch, seq, d), ys.reshape(db, dseq, d),
            rows(kf, 0, batch, seq), rows(vf, 0, batch, seq), lf[:tp].reshape(depth, batch, seq, fh),
            rows(km, 0, batch, seq), rows(vm, 0, batch, seq),
            rows(kf, tp, db, dseq), rows(vf, tp, db, dseq), lf[tp:].reshape(depth, db, dseq, fh),
            rows(km, tp, db, dseq), rows(vm, tp, db, dseq))
```

```python
import functools
import math

import jax
import jax.numpy as jnp
from jax import lax
from jax.experimental import pallas as pl
from jax.experimental.pallas import tpu as pltpu

f32 = jnp.float32
bf16 = jnp.bfloat16

MOBA_BLOCK = 256
MOBA_TOPK = 3
ROPE_THETA = 10000.0
RMS_EPS = 1e-6
LANES = 128
SUBLANES = 8
BF16_ROWS = 16
ROW_TILE = 256
MM_ROWS = 1664
MM_ROWS_WIDE_K = 1040
PROJ_ROWS = 832
MM_COLS = 256
MM_COLS_WIDE = 512
VMEM_CAP = 60 * 1024 * 1024

_NT = (((1,), (1,)), ((), ()))


def _nbytes(shape, dtype):
    n = 1
    for s in shape:
        n *= s
    return n * jnp.dtype(dtype).itemsize


def _params(sem, pipelined, resident=0):
    est = 2 * sum(_nbytes(s, d) for s, d in pipelined) + resident
    limit = min(VMEM_CAP, max(32 * 1024 * 1024, est + est // 4))
    return pltpu.CompilerParams(dimension_semantics=sem, vmem_limit_bytes=int(limit))


def _divisor_tile(t, cap, mult):
    best = None
    for c in range(mult, min(t, cap) + 1, mult):
        if t % c == 0:
            best = c
    assert best is not None, (t, cap, mult)
    return best


def _rmsnorm(x, g):
    ms = jnp.mean(x * x, axis=-1, keepdims=True)
    return x * lax.rsqrt(ms + RMS_EPS) * g


def _part_tiles(parts, tr):
    spans, start = [], 0
    for n, a in enumerate(parts):
        assert a.shape[0] % tr == 0 or n == len(parts) - 1
        count = pl.cdiv(a.shape[0], tr)
        spans.append((start, count))
        start += count
    return spans, start


def _part_spec(span, rows, tr, width):
    first, count = span
    return pl.BlockSpec((min(tr, rows), width), lambda i: (jnp.clip(i - first, 0, count - 1), 0))


def _read_parts(refs, spans, tr):
    i = pl.program_id(0)
    x = None
    for r, (first, _) in zip(refs, spans):
        v = r[...]
        if v.shape[0] < tr:
            v = jnp.concatenate([v, jnp.zeros((tr - v.shape[0], v.shape[1]), v.dtype)], axis=0)
        x = v if x is None else jnp.where(i >= first, v, x)
    return x


def _write_parts(refs, spans, x):
    i = pl.program_id(0)
    for r, (first, count) in zip(refs, spans):
        @pl.when((i >= first) & (i < first + count))
        def _(r=r):
            r[...] = x[:r.shape[0]].astype(r.dtype)


def _row_tile(parts):
    return min(ROW_TILE, sum(a.shape[0] for a in parts))


def _prenorm_kernel(*refs, spans):
    n = len(spans)
    g_ref, o_ref = refs[n], refs[n + 1]
    o_ref[...] = _rmsnorm(_read_parts(refs[:n], spans, o_ref.shape[0]), g_ref[...]).astype(o_ref.dtype)


def _prenorm(x_parts, g):
    d = x_parts[0].shape[1]
    tr = _row_tile(x_parts)
    spans, tiles = _part_tiles(x_parts, tr)
    return pl.pallas_call(
        functools.partial(_prenorm_kernel, spans=spans),
        grid=(tiles,),
        in_specs=[_part_spec(s, a.shape[0], tr, d) for s, a in zip(spans, x_parts)]
        + [pl.BlockSpec((1, d), lambda i: (0, 0))],
        out_specs=pl.BlockSpec((tr, d), lambda i: (i, 0)),
        out_shape=jax.ShapeDtypeStruct((sum(a.shape[0] for a in x_parts), d), bf16),
        compiler_params=_params(("arbitrary",), [((tr, d), f32)] * len(spans) + [((tr, d), bf16)],
                                2 * _nbytes((tr, d), f32)),
        name="prenorm",
    )(*x_parts, g.reshape(1, d))


def _resnorm_kernel(*refs, res_spans, out_spans, weight, has_next):
    y_ref = refs[0]
    nres = len(res_spans)
    res_refs = refs[1:1 + nres]
    gp_ref = refs[1 + nres]
    rest = refs[2 + nres:]
    y = _rmsnorm(y_ref[...], gp_ref[...])
    if weight != 1.0:
        y = weight * y
    h = _read_parts(res_refs, res_spans, y.shape[0]) + y
    if has_next:
        gn_ref, h_ref, hn_ref = rest
        h_ref[...] = h
        hn_ref[...] = _rmsnorm(h, gn_ref[...]).astype(hn_ref.dtype)
    else:
        _write_parts(rest, out_spans, h)


def _resnorm(y, res_parts, g_post, weight, g_next=None, out_rows=None):
    t, d = y.shape
    tr = _row_tile(res_parts)
    res_spans, tiles = _part_tiles(res_parts, tr)
    assert sum(a.shape[0] for a in res_parts) == t
    row = pl.BlockSpec((tr, d), lambda i: (i, 0))
    vec = pl.BlockSpec((1, d), lambda i: (0, 0))
    has_next = g_next is not None
    in_specs = ([row] + [_part_spec(s, a.shape[0], tr, d) for s, a in zip(res_spans, res_parts)] + [vec]
                + ([vec] if has_next else []))
    args = [y, *res_parts, g_post.reshape(1, d)] + ([g_next.reshape(1, d)] if has_next else [])
    if has_next:
        out_spans = None
        out_specs = [row, row]
        out_shape = [jax.ShapeDtypeStruct((t, d), f32), jax.ShapeDtypeStruct((t, d), bf16)]
    else:
        out_rows = out_rows or [t]
        assert sum(out_rows) == t
        out_shape = [jax.ShapeDtypeStruct((r, d), f32) for r in out_rows]
        out_spans, _ = _part_tiles(out_shape, tr)
        out_specs = [_part_spec(s, r, tr, d) for s, r in zip(out_spans, out_rows)]
    return pl.pallas_call(
        functools.partial(_resnorm_kernel, res_spans=res_spans, out_spans=out_spans, weight=weight, has_next=has_next),
        grid=(tiles,),
        in_specs=in_specs,
        out_specs=out_specs,
        out_shape=out_shape,
        compiler_params=_params(("arbitrary",), [((tr, d), f32)] * 5, 3 * _nbytes((tr, d), f32)),
        name="resnorm",
    )(*args)


def _groupnorm_kernel(*refs, spans):
    n = len(spans)
    of_refs, om_refs = refs[:n], refs[n:2 * n]
    gf_ref, gm_ref, z_ref = refs[2 * n:]
    fw = of_refs[0].shape[1]
    tr = z_ref.shape[0]
    z_ref[:, :fw] = _rmsnorm(_read_parts(of_refs, spans, tr), gf_ref[...]).astype(z_ref.dtype)
    z_ref[:, fw:] = _rmsnorm(_read_parts(om_refs, spans, tr), gm_ref[...]).astype(z_ref.dtype)


def _groupnorm(of_parts, om_parts, g_f, g_m):
    fw, mw = of_parts[0].shape[1], om_parts[0].shape[1]
    tr = _row_tile(of_parts)
    spans, tiles = _part_tiles(of_parts, tr)
    return pl.pallas_call(
        functools.partial(_groupnorm_kernel, spans=spans),
        grid=(tiles,),
        in_specs=[_part_spec(s, a.shape[0], tr, fw) for s, a in zip(spans, of_parts)]
        + [_part_spec(s, a.shape[0], tr, mw) for s, a in zip(spans, om_parts)]
        + [pl.BlockSpec((1, fw), lambda i: (0, 0)), pl.BlockSpec((1, mw), lambda i: (0, 0))],
        out_specs=pl.BlockSpec((tr, fw + mw), lambda i: (i, 0)),
        out_shape=jax.ShapeDtypeStruct((sum(a.shape[0] for a in of_parts), fw + mw), bf16),
        compiler_params=_params(("arbitrary",), [((tr, fw + mw), f32)] * len(spans) + [((tr, fw + mw), bf16)],
                                2 * _nbytes((tr, fw + mw), f32)),
        name="groupnorm",
    )(*of_parts, *om_parts, g_f.reshape(1, fw), g_m.reshape(1, mw))


def _prefix_lanes(x):
    lane = lax.broadcasted_iota(jnp.int32, x.shape, 1)
    sh = 1
    while sh < x.shape[1]:
        x = x + jnp.where(lane >= sh, pltpu.roll(x, sh, 1), 0.0)
        sh *= 2
    return x


def _cumsum_kernel(lf_ref, c_ref, *, group, inv_scale):
    c = _prefix_lanes(lf_ref[...].T)
    bt = (c * (-inv_scale)).T
    pad = jnp.zeros((bt.shape[0], LANES - 3 * group), f32)
    for kv in range(c_ref.shape[1]):
        b = bt[:, kv * group:(kv + 1) * group]
        hi = b.astype(bf16).astype(f32)
        mid = (b - hi).astype(bf16).astype(f32)
        lo = b - hi - mid
        c_ref[0, kv] = jnp.concatenate([hi, mid, lo, pad], axis=1).astype(bf16)


def _cumsum_logf(logf, *, batch, seq, kvh, scale):
    fh = logf.shape[1]
    g = fh // kvh
    assert 3 * g <= LANES
    return pl.pallas_call(
        functools.partial(_cumsum_kernel, group=g, inv_scale=1.0 / scale),
        grid=(batch,),
        in_specs=[pl.BlockSpec((seq, fh), lambda i: (i, 0))],
        out_specs=pl.BlockSpec((1, kvh, seq, LANES), lambda i: (i, 0, 0, 0)),
        out_shape=jax.ShapeDtypeStruct((batch, kvh, seq, LANES), bf16),
        compiler_params=_params(("arbitrary",), [((seq, LANES), f32), ((kvh * seq, LANES), bf16)],
                                8 * _nbytes((seq, LANES), f32)),
        name="cumsum_logf",
    )(logf)


def _gate_up_kernel(x_ref, wg_ref, wu_ref, o_ref):
    x = x_ref[...]
    g = jnp.dot(x, wg_ref[...].astype(bf16), preferred_element_type=f32)
    u = jnp.dot(x, wu_ref[...].astype(bf16), preferred_element_type=f32)
    o_ref[...] = (jax.nn.silu(g) * u).astype(o_ref.dtype)


def _gate_up(xn, w_gate, w_up):
    t, d = xn.shape
    ff = w_gate.shape[1]
    tm = _divisor_tile(t, MM_ROWS, BF16_ROWS)
    tn = MM_COLS
    return pl.pallas_call(
        _gate_up_kernel,
        grid=(t // tm, pl.cdiv(ff, tn)),
        in_specs=[pl.BlockSpec((tm, d), lambda i, j: (i, 0), pipeline_mode=pl.Buffered(1)),
                  pl.BlockSpec((d, tn), lambda i, j: (0, j)),
                  pl.BlockSpec((d, tn), lambda i, j: (0, j))],
        out_specs=pl.BlockSpec((tm, tn), lambda i, j: (i, j)),
        out_shape=jax.ShapeDtypeStruct((t, ff), bf16),
        compiler_params=_params(("arbitrary", "arbitrary"),
                                [((d, tn), f32), ((d, tn), f32), ((tm, tn), bf16)],
                                _nbytes((tm, d), bf16) + 2 * _nbytes((d, tn), bf16) + 4 * _nbytes((tm, tn), f32)),
        name="gate_up",
    )(xn, w_gate, w_up)


def _matmul_kernel(x_ref, w_ref, o_ref):
    o_ref[...] = jnp.dot(x_ref[...], w_ref[...].astype(bf16), preferred_element_type=f32)


def _matmul(x, w, tn):
    t, k = x.shape
    n = w.shape[1]
    tm = _divisor_tile(t, MM_ROWS if k * tn <= 4096 * MM_COLS_WIDE else MM_ROWS_WIDE_K, BF16_ROWS)
    return pl.pallas_call(
        _matmul_kernel,
        grid=(t // tm, pl.cdiv(n, tn)),
        in_specs=[pl.BlockSpec((tm, k), lambda i, j: (i, 0), pipeline_mode=pl.Buffered(1)),
                  pl.BlockSpec((k, tn), lambda i, j: (0, j))],
        out_specs=pl.BlockSpec((tm, tn), lambda i, j: (i, j)),
        out_shape=jax.ShapeDtypeStruct((t, n), f32),
        compiler_params=_params(("arbitrary", "arbitrary"),
                                [((k, tn), f32), ((tm, tn), f32)],
                                _nbytes((tm, k), bf16) + _nbytes((k, tn), bf16) + 2 * _nbytes((tm, tn), f32)),
        name="matmul",
    )(x, w)


def _proj_kernel(x_ref, wt_ref, cos_ref, sin_ref, b_ref, *out_refs, bounds, hd, fh):
    j = pl.program_id(1)

    def product(cols=None):
        w = wt_ref[...] if cols is None else wt_ref[:cols, :]
        return lax.dot_general(x_ref[...], w.astype(bf16), _NT, preferred_element_type=f32)

    def rope(ref):
        acc = product()
        cosf, sinf = cos_ref[...], sin_ref[...]
        for h in range(acc.shape[1] // hd):
            x = acc[:, h * hd:(h + 1) * hd]
            ref[:, h * hd:(h + 1) * hd] = (x * cosf + pltpu.roll(x, hd // 2, 1) * sinf).astype(ref.dtype)

    def plain(ref):
        ref[...] = product().astype(ref.dtype)

    def forget(ref):
        ref[...] = jax.nn.log_sigmoid(product(LANES)[:, :fh] + b_ref[...])

    actions = [plain, plain, plain, rope, rope, plain, forget]
    for ref, act, (lo, hi) in zip(out_refs, actions, bounds):
        @pl.when((j >= lo) & (j < hi))
        def _(ref=ref, act=act):
            act(ref)


def _proj(xn, w_in_t, cosf, sinf, b_forget, *, fw, kvw, mw, fh, hd):
    t, d = xn.shape
    tn = math.gcd(MM_COLS_WIDE, fw, kvw, mw)
    tm = _divisor_tile(t, PROJ_ROWS, BF16_ROWS)
    widths = [fw, kvw, kvw, mw, kvw, kvw]
    bounds, lo = [], 0
    for w in widths:
        bounds.append((lo, lo + w // tn))
        lo += w // tn
    bounds.append((lo, lo + 1))
    assert lo * tn + fh == w_in_t.shape[0] and fh <= tn

    def out_spec(width, lo, hi):
        return pl.BlockSpec((tm, width), lambda i, j: (i, jnp.clip(j - lo, 0, hi - lo - 1)))

    out_specs = [out_spec(tn, lo, hi) for lo, hi in bounds[:-1]] + [pl.BlockSpec((tm, fh), lambda i, j: (i, 0))]
    dtypes = [bf16, f32, f32, bf16, f32, f32, f32]
    out_shape = [jax.ShapeDtypeStruct((t, w), dt) for w, dt in zip(widths + [fh], dtypes)]
    return pl.pallas_call(
        functools.partial(_proj_kernel, bounds=bounds, hd=hd, fh=fh),
        grid=(t // tm, bounds[-1][1]),
        in_specs=[pl.BlockSpec((tm, d), lambda i, j: (i, 0), pipeline_mode=pl.Buffered(1)),
                  pl.BlockSpec((tn, d), lambda i, j: (j, 0)),
                  pl.BlockSpec((tm, hd), lambda i, j: (i, 0)),
                  pl.BlockSpec((tm, hd), lambda i, j: (i, 0)),
                  pl.BlockSpec((1, fh), lambda i, j: (0, 0))],
        out_specs=out_specs,
        out_shape=out_shape,
        compiler_params=_params(("arbitrary", "arbitrary"),
                                [((tn, d), f32)] + [((tm, tn), f32)] * 7 + [((tm, hd), f32)] * 2,
                                _nbytes((tm, d), bf16) + _nbytes((tn, d), bf16) + 3 * _nbytes((tm, tn), f32)),
        name="proj",
    )(xn, w_in_t, cosf, sinf, b_forget.reshape(1, fh))


MAX_MOBA_BLOCKS = LANES // 2


def _rank_select(gate, n_valid, nblk, axis):
    idx = lax.broadcasted_iota(jnp.int32, gate.shape, axis)
    valid = idx < n_valid
    g = jnp.where(valid, gate, -jnp.inf)
    rank = jnp.zeros(gate.shape, f32)
    size = gate.shape[axis]
    assert size >= 2 * nblk
    for d in range(1, nblk):
        lower = pltpu.roll(g, d, axis)
        upper = pltpu.roll(g, size - d, axis)
        rank = rank + (lower >= g).astype(f32) + (upper > g).astype(f32)
    return jnp.where(valid & (rank < MOBA_TOPK), 0.0, -jnp.inf)


def _attn_kernel(*refs, kind, group, tq, nblk, scale):
    if kind == "fox":
        q_ref, k_ref, v_ref, c_ref, o_ref, kb_ref, vt_ref, qa_ref, m_ref, acc_ref, s_ref = refs
    else:
        q_ref, k_ref, v_ref, o_ref, kb_ref, vt_ref, qa_ref, m_ref, acc_ref, s_ref, km_ref, sel_ref = refs
    qi = pl.program_id(2)
    hd = k_ref.shape[-1]
    rows = group * tq

    @pl.when(qi == 0)
    def _():
        for n in range(nblk):
            kn = k_ref[n * tq:(n + 1) * tq, :]
            kb_ref[n, :, :hd] = kn.astype(bf16)
            if kind == "fox":
                kb_ref[n, :, hd:] = c_ref[0, 0, n * tq:(n + 1) * tq, :]
            vt_ref[n, :hd, :] = v_ref[n * tq:(n + 1) * tq, :].T.astype(bf16)
            vt_ref[n, hd:, :] = jnp.ones((BF16_ROWS, tq), bf16)
        if kind == "moba":
            means = [jnp.mean(k_ref[n * tq:(n + 1) * tq, :], axis=0, keepdims=True) for n in range(nblk)]
            km_ref[...] = jnp.concatenate(means + [jnp.zeros((LANES - nblk, hd), f32)], axis=0).astype(bf16)

    q = q_ref[...]
    for g in range(group):
        qa_ref[g * tq:(g + 1) * tq, :hd] = q[:, g * hd:(g + 1) * hd]
    lane_q = lax.broadcasted_iota(jnp.int32, (tq, rows), 1)
    if kind == "fox":
        lane = lax.broadcasted_iota(jnp.int32, (rows, hd), 1)
        head = lax.broadcasted_iota(jnp.int32, (rows, hd), 0) // tq
        pick = (lane == head) | (lane == head + group) | (lane == head + 2 * group)
        qa_ref[:, hd:] = jnp.where(pick, 1.0, 0.0).astype(bf16)
    else:
        gate = lax.dot_general(km_ref[...], qa_ref[...], _NT, preferred_element_type=f32)
        sel_ref[...] = _rank_select(gate[:sel_ref.shape[0]], qi, nblk, 0)

    m_ref[...] = jnp.full(m_ref.shape, -jnp.inf, f32)
    acc_ref[...] = jnp.zeros(acc_ref.shape, f32)

    def scores(j, diagonal):
        s = lax.dot_general(kb_ref[j], qa_ref[...], _NT, preferred_element_type=f32) * scale
        if diagonal:
            key = lax.broadcasted_iota(jnp.int32, s.shape, 0)
            s = jnp.where(key <= (lane_q & (tq - 1)), s, -jnp.inf)
        elif kind == "moba":
            s = s + sel_ref[pl.ds(j, 1), :]
        return s

    def absorb(s, j):
        m_prev = m_ref[...]
        m_new = jnp.maximum(m_prev, jnp.max(s, axis=0, keepdims=True))
        alpha = jnp.exp(m_prev - m_new)
        p = jnp.exp(s - m_new).astype(bf16)
        acc_ref[...] = alpha * acc_ref[...] + jnp.dot(vt_ref[j], p, preferred_element_type=f32)
        m_ref[...] = m_new

    def prev_block(t):
        return jnp.where(t == 0, qi, t - 1)

    s_ref[0] = scores(qi, True)

    def pair(i, carry):
        t = 2 * i
        s_ref[1] = scores(t, False)
        absorb(s_ref[0], prev_block(t))
        s_ref[0] = scores(t + 1, False)
        absorb(s_ref[1], t)
        return carry

    lax.fori_loop(0, lax.shift_right_logical(qi, 1), pair, 0)

    @pl.when((qi & 1) == 1)
    def _():
        t = qi - 1
        s_ref[1] = scores(t, False)
        absorb(s_ref[0], prev_block(t))
        absorb(s_ref[1], t)

    @pl.when((qi & 1) == 0)
    def _():
        absorb(s_ref[0], prev_block(qi))

    acc = acc_ref[...]
    o = (acc[:hd] / acc[hd:hd + 1]).T
    for g in range(group):
        o_ref[:, g * hd:(g + 1) * hd] = o[g * tq:(g + 1) * tq, :]


def _prompt_attention(kind, q, k, v, caug, *, batch, seq, kvh, hd):
    qw = q.shape[1]
    group = qw // (kvh * hd)
    tq = MOBA_BLOCK
    assert tq & (tq - 1) == 0
    nq = seq // tq
    rows = group * tq
    scale = hd ** -0.5
    width = 2 * hd if kind == "fox" else hd
    q_spec = pl.BlockSpec((tq, group * hd), lambda b, h, i: (b * nq + i, h))
    kv_spec = pl.BlockSpec((seq, hd), lambda b, h, i: (b, h))
    scratch = [pltpu.VMEM((nq, tq, width), bf16), pltpu.VMEM((nq, hd + BF16_ROWS, tq), bf16),
               pltpu.VMEM((rows, width), bf16), pltpu.VMEM((1, rows), f32), pltpu.VMEM((hd + BF16_ROWS, rows), f32),
               pltpu.VMEM((2, tq, rows), f32)]
    in_specs = [q_spec, kv_spec, kv_spec]
    args = [q, k, v]
    if kind == "fox":
        in_specs.append(pl.BlockSpec((1, 1, seq, LANES), lambda b, h, i: (b, h, 0, 0)))
        args.append(caug)
    else:
        assert nq <= MAX_MOBA_BLOCKS
        sel_rows = -(-2 * nq // SUBLANES) * SUBLANES
        scratch += [pltpu.VMEM((LANES, hd), bf16), pltpu.VMEM((sel_rows, rows), f32)]
    return pl.pallas_call(
        functools.partial(_attn_kernel, kind=kind, group=group, tq=tq, nblk=nq, scale=scale),
        grid=(batch, kvh, nq),
        in_specs=in_specs,
        out_specs=q_spec,
        out_shape=jax.ShapeDtypeStruct((batch * seq, qw), f32),
        scratch_shapes=scratch,
        compiler_params=_params(("arbitrary", "arbitrary", "arbitrary"),
                                [((tq, group * hd), bf16), ((seq, hd), f32), ((seq, hd), f32),
                                 ((tq, group * hd), f32), ((seq, LANES), bf16)],
                                _nbytes((seq, 3 * hd), bf16) + 8 * _nbytes((tq, rows), f32)),
        name=kind + "_attention",
    )(*args)


def _block_diag_q(q, kvh):
    heads, hd = q.shape
    group = heads // kvh
    qt = jnp.concatenate([q] * kvh, axis=1)
    row_kv = lax.broadcasted_iota(jnp.int32, qt.shape, 0) // group
    col_kv = lax.broadcasted_iota(jnp.int32, qt.shape, 1) // hd
    return jnp.where(row_kv == col_kv, qt, jnp.zeros_like(qt))


def _own_kv_slab(x, kvh):
    heads, w = x.shape
    hd = w // kvh
    group = heads // kvh
    row_kv = lax.broadcasted_iota(jnp.int32, (heads, hd), 0) // group
    out = jnp.zeros((heads, hd), x.dtype)
    for g in range(kvh):
        out = out + jnp.where(row_kv == g, x[:, g * hd:(g + 1) * hd], 0.0)
    return out


def _page_rows(ref, kvh):
    page = ref.shape[1] // kvh
    return jnp.concatenate([ref[0, pl.ds(h, page, stride=kvh), :] for h in range(kvh)], axis=1)


def _decode_kernel(pt_ref, q_ref, *refs, kind, npages, kvh, ppb, scale):
    k_refs = refs[:npages]
    v_refs = refs[npages:2 * npages]
    if kind == "fox":
        lf_refs = refs[2 * npages:3 * npages]
        kn_ref, vn_ref, lfn_ref, o_ref = refs[3 * npages:]
    else:
        kn_ref, vn_ref, o_ref = refs[2 * npages:]
    page = k_refs[0].shape[1] // kvh
    qbd = _block_diag_q(q_ref[0], kvh)
    kn = kn_ref[0].astype(bf16).astype(f32)
    vn = vn_ref[0].astype(bf16).astype(f32)
    s_new = jnp.sum(qbd.astype(f32) * kn, axis=1, keepdims=True) * scale

    k_pages = [_page_rows(r, kvh) for r in k_refs]
    s = jnp.concatenate([lax.dot_general(qbd, kp.astype(bf16), _NT, preferred_element_type=f32) for kp in k_pages],
                        axis=1) * scale
    if kind == "fox":
        c_in = _prefix_lanes(jnp.concatenate([r[0] for r in lf_refs], axis=1))
        s = s - c_in
        s_new = s_new - (c_in[:, c_in.shape[1] - 1:] + lfn_ref[0])
    else:
        nblk = npages // ppb
        sums = [sum(jnp.sum(k_pages[n * ppb + i], axis=0, keepdims=True) for i in range(ppb)) for n in range(nblk)]
        kmean = jnp.concatenate(sums + [jnp.zeros((LANES - nblk, qbd.shape[1]), f32)], axis=0) / (ppb * page)
        gate = lax.dot_general(qbd, kmean.astype(bf16), _NT, preferred_element_type=f32)
        sel = _rank_select(gate, nblk, nblk, 1)
        s = s + jnp.concatenate([jnp.broadcast_to(sel[:, n:n + 1], (sel.shape[0], ppb * page)) for n in range(nblk)],
                                axis=1)
    m = jnp.maximum(jnp.max(s, axis=1, keepdims=True), s_new)
    p = jnp.exp(s - m)
    p_new = jnp.exp(s_new - m)
    l = jnp.sum(p, axis=1, keepdims=True) + p_new
    acc = p_new * vn
    for i in range(npages):
        acc = acc + jnp.dot(p[:, i * page:(i + 1) * page].astype(bf16), _page_rows(v_refs[i], kvh).astype(bf16),
                            preferred_element_type=f32)
    o_ref[0] = _own_kv_slab(acc / l, kvh)


def _decode_attention(kind, q, k_pool, v_pool, lf_pool, k_new, v_new, lf_new, page_table, *, kvh, hd):
    db, npages = page_table.shape
    heads = q.shape[1] // hd
    kvw = kvh * hd
    page = k_pool.shape[1]
    scale = hd ** -0.5
    ppb = MOBA_BLOCK // page
    assert npages % ppb == 0 and 2 * (npages // ppb) <= LANES
    q3 = q.reshape(db, heads, hd)
    k_flat = k_pool.reshape(k_pool.shape[0], page * kvh, hd)
    v_flat = v_pool.reshape(v_pool.shape[0], page * kvh, hd)

    def paged_kv(i):
        return pl.BlockSpec((1, page * kvh, hd), lambda b, pt: (pt[b * npages + i], 0, 0))

    per_seq = lambda shape: pl.BlockSpec((1,) + shape, lambda b, pt: (b, 0, 0))
    in_specs = [per_seq((heads, hd))] + [paged_kv(i) for i in range(npages)] * 2
    args = [q3] + [k_flat] * npages + [v_flat] * npages
    if kind == "fox":
        lf_t = jnp.swapaxes(lf_pool, 1, 2)
        in_specs += [pl.BlockSpec((1, heads, page), lambda b, pt, i=i: (pt[b * npages + i], 0, 0))
                     for i in range(npages)]
        args += [lf_t] * npages
        in_specs += [per_seq((1, kvw)), per_seq((1, kvw)), per_seq((heads, 1))]
        args += [k_new.reshape(db, 1, kvw), v_new.reshape(db, 1, kvw), lf_new.reshape(db, heads, 1)]
    else:
        in_specs += [per_seq((1, kvw)), per_seq((1, kvw))]
        args += [k_new.reshape(db, 1, kvw), v_new.reshape(db, 1, kvw)]
    out = pl.pallas_call(
        functools.partial(_decode_kernel, kind=kind, npages=npages, kvh=kvh, ppb=ppb, scale=scale),
        grid_spec=pltpu.PrefetchScalarGridSpec(
            num_scalar_prefetch=1,
            grid=(db,),
            in_specs=in_specs,
            out_specs=pl.BlockSpec((1, heads, hd), lambda b, pt: (b, 0, 0)),
        ),
        out_shape=jax.ShapeDtypeStruct((db, heads, hd), f32),
        compiler_params=_params(("arbitrary",), [((page, kvw), f32)] * (2 * npages),
                                6 * npages * _nbytes((page, kvw), bf16)),
        name=kind + "_decode",
    )(page_table.reshape(-1), *args)
    return out.reshape(db, heads * hd)


def _rope_tables(pos, hd):
    half = hd // 2
    inv_freq = ROPE_THETA ** (-jnp.arange(half, dtype=f32) / half)
    ang = pos.astype(f32)[:, None] * inv_freq[None, :]
    cos, sin = jnp.cos(ang), jnp.sin(ang)
    return jnp.concatenate([cos, cos], axis=-1), jnp.concatenate([-sin, sin], axis=-1)


def kernel(x_prompt, x_sample, cache_fox_k, cache_fox_v, cache_fox_logf, cache_moba_k, cache_moba_v, page_table,
           g_pre_ffn1, g_post_ffn1, w1_gate, w1_up, w1_down, g_pre_mix, w_in, b_forget, g_out_fox, g_out_moba,
           w_out, g_post_mix, g_pre_ffn2, g_post_ffn2, w2_gate, w2_up, w2_down):
    depth = w_in.shape[0]
    assert depth == 1, "one layer: the sample group reads the caches the prompt group does not write"
    batch, seq, d = x_prompt.shape
    db, dseq, _ = x_sample.shape
    assert dseq == 1
    kvh, hd = cache_fox_k.shape[3], cache_fox_k.shape[4]
    assert hd == LANES
    kvw = kvh * hd
    fh = b_forget.shape[-1]
    fw = fh * hd
    mw = w_in.shape[-1] - fw - 4 * kvw - fh
    page = cache_fox_k.shape[2]
    past_len = page_table.shape[1] * page
    assert seq % MOBA_BLOCK == 0 and past_len % MOBA_BLOCK == 0
    tp = batch * seq

    def layer0(a):
        return a.reshape(a.shape[1:])

    def half_ffn(res_parts, hn, g_post, w_gate, w_up, w_down, **kw):
        a = _gate_up(hn, layer0(w_gate), layer0(w_up))
        y = _matmul(a, layer0(w_down), MM_COLS)
        return _resnorm(y, res_parts, layer0(g_post), 0.5, **kw)

    x_parts = [x_prompt.reshape(tp, d), x_sample.reshape(db, d)]
    hn = _prenorm(x_parts, layer0(g_pre_ffn1))
    h, hn = half_ffn(x_parts, hn, g_post_ffn1, w1_gate, w1_up, w1_down, g_next=layer0(g_pre_mix))

    pos = jnp.concatenate([jnp.tile(jnp.arange(seq, dtype=jnp.int32), batch), jnp.full((db,), past_len, jnp.int32)])
    cosf, sinf = _rope_tables(pos, hd)
    w_in_t = jnp.swapaxes(layer0(w_in), 0, 1)
    qf, kf, vf, qm, km, vm, lf = _proj(hn, w_in_t, cosf, sinf, layer0(b_forget), fw=fw, kvw=kvw, mw=mw, fh=fh, hd=hd)

    scale = hd ** -0.5
    caug = _cumsum_logf(lf, batch=batch, seq=seq, kvh=kvh, scale=scale)
    of_p = _prompt_attention("fox", qf, kf, vf, caug, batch=batch, seq=seq, kvh=kvh, hd=hd)
    om_p = _prompt_attention("moba", qm, km, vm, None, batch=batch, seq=seq, kvh=kvh, hd=hd)
    of_s = _decode_attention("fox", qf[tp:], layer0(cache_fox_k), layer0(cache_fox_v), layer0(cache_fox_logf),
                             kf[tp:], vf[tp:], lf[tp:], page_table, kvh=kvh, hd=hd)
    om_s = _decode_attention("moba", qm[tp:], layer0(cache_moba_k), layer0(cache_moba_v), None,
                             km[tp:], vm[tp:], None, page_table, kvh=kvh, hd=hd)

    z = _groupnorm([of_p, of_s], [om_p, om_s], layer0(g_out_fox), layer0(g_out_moba))
    y = _matmul(z, layer0(w_out), MM_COLS_WIDE)
    h, hn = _resnorm(y, [h], layer0(g_post_mix), 1.0, g_next=layer0(g_pre_ffn2))
    yp, ys = half_ffn([h], hn, g_post_ffn2, w2_gate, w2_up, w2_down, out_rows=[tp, db])

    def rows(a, lo, b, s):
        return a[lo:lo + b * s].reshape(depth, b, s, kvh, hd)

    return (yp.reshape(batch, seq, d), ys.reshape(db, dseq, d),
            rows(kf, 0, batch, seq), rows(vf, 0, batch, seq), lf[:tp].reshape(depth, batch, seq, fh),
            rows(km, 0, batch, seq), rows(vm, 0, batch, seq),
            rows(kf, tp, db, dseq), rows(vf, tp, db, dseq), lf[tp:].reshape(depth, db, dseq, fh),
            rows(km, tp, db, dseq), rows(vm, tp, db, dseq))
```

```python
import functools
import math

import jax
import jax.numpy as jnp
from jax import lax
from jax.experimental import pallas as pl
from jax.experimental.pallas import tpu as pltpu

f32 = jnp.float32
bf16 = jnp.bfloat16

MOBA_BLOCK = 256
MOBA_TOPK = 3
ROPE_THETA = 10000.0
RMS_EPS = 1e-6
LANES = 128
SUBLANES = 8
BF16_ROWS = 16
ROW_TILE = 256
MM_ROWS = 1664
MM_ROWS_WIDE_K = 1040
PROJ_ROWS = 1040
MM_COLS = 256
MM_COLS_WIDE = 512
VMEM_CAP = 60 * 1024 * 1024

_NT = (((1,), (1,)), ((), ()))


def _nbytes(shape, dtype):
    n = 1
    for s in shape:
        n *= s
    return n * jnp.dtype(dtype).itemsize


def _params(sem, pipelined, resident=0):
    est = 2 * sum(_nbytes(s, d) for s, d in pipelined) + resident
    limit = min(VMEM_CAP, max(32 * 1024 * 1024, est + est // 4))
    return pltpu.CompilerParams(dimension_semantics=sem, vmem_limit_bytes=int(limit))


def _divisor_tile(t, cap, mult):
    best = None
    for c in range(mult, min(t, cap) + 1, mult):
        if t % c == 0:
            best = c
    assert best is not None, (t, cap, mult)
    return best


def _rmsnorm(x, g):
    ms = jnp.mean(x * x, axis=-1, keepdims=True)
    return x * lax.rsqrt(ms + RMS_EPS) * g


def _part_tiles(parts, tr):
    spans, start = [], 0
    for n, a in enumerate(parts):
        assert a.shape[0] % tr == 0 or n == len(parts) - 1
        count = pl.cdiv(a.shape[0], tr)
        spans.append((start, count))
        start += count
    return spans, start


def _part_spec(span, rows, tr, width):
    first, count = span
    return pl.BlockSpec((min(tr, rows), width), lambda i: (jnp.clip(i - first, 0, count - 1), 0))


def _read_parts(refs, spans, tr):
    i = pl.program_id(0)
    x = None
    for r, (first, _) in zip(refs, spans):
        v = r[...]
        if v.shape[0] < tr:
            v = jnp.concatenate([v, jnp.zeros((tr - v.shape[0], v.shape[1]), v.dtype)], axis=0)
        x = v if x is None else jnp.where(i >= first, v, x)
    return x


def _write_parts(refs, spans, x):
    i = pl.program_id(0)
    for r, (first, count) in zip(refs, spans):
        @pl.when((i >= first) & (i < first + count))
        def _(r=r):
            r[...] = x[:r.shape[0]].astype(r.dtype)


def _row_tile(parts):
    return min(ROW_TILE, sum(a.shape[0] for a in parts))


def _prenorm_kernel(*refs, spans):
    n = len(spans)
    g_ref, o_ref = refs[n], refs[n + 1]
    o_ref[...] = _rmsnorm(_read_parts(refs[:n], spans, o_ref.shape[0]), g_ref[...]).astype(o_ref.dtype)


def _prenorm(x_parts, g):
    d = x_parts[0].shape[1]
    tr = _row_tile(x_parts)
    spans, tiles = _part_tiles(x_parts, tr)
    return pl.pallas_call(
        functools.partial(_prenorm_kernel, spans=spans),
        grid=(tiles,),
        in_specs=[_part_spec(s, a.shape[0], tr, d) for s, a in zip(spans, x_parts)]
        + [pl.BlockSpec((1, d), lambda i: (0, 0))],
        out_specs=pl.BlockSpec((tr, d), lambda i: (i, 0)),
        out_shape=jax.ShapeDtypeStruct((sum(a.shape[0] for a in x_parts), d), bf16),
        compiler_params=_params(("arbitrary",), [((tr, d), f32)] * len(spans) + [((tr, d), bf16)],
                                2 * _nbytes((tr, d), f32)),
        name="prenorm",
    )(*x_parts, g.reshape(1, d))


def _resnorm_kernel(*refs, res_spans, out_spans, weight, has_next):
    y_ref = refs[0]
    nres = len(res_spans)
    res_refs = refs[1:1 + nres]
    gp_ref = refs[1 + nres]
    rest = refs[2 + nres:]
    y = _rmsnorm(y_ref[...], gp_ref[...])
    if weight != 1.0:
        y = weight * y
    h = _read_parts(res_refs, res_spans, y.shape[0]) + y
    if has_next:
        gn_ref, h_ref, hn_ref = rest
        h_ref[...] = h
        hn_ref[...] = _rmsnorm(h, gn_ref[...]).astype(hn_ref.dtype)
    else:
        _write_parts(rest, out_spans, h)


def _resnorm(y, res_parts, g_post, weight, g_next=None, out_rows=None):
    t, d = y.shape
    tr = _row_tile(res_parts)
    res_spans, tiles = _part_tiles(res_parts, tr)
    assert sum(a.shape[0] for a in res_parts) == t
    row = pl.BlockSpec((tr, d), lambda i: (i, 0))
    vec = pl.BlockSpec((1, d), lambda i: (0, 0))
    has_next = g_next is not None
    in_specs = ([row] + [_part_spec(s, a.shape[0], tr, d) for s, a in zip(res_spans, res_parts)] + [vec]
                + ([vec] if has_next else []))
    args = [y, *res_parts, g_post.reshape(1, d)] + ([g_next.reshape(1, d)] if has_next else [])
    if has_next:
        out_spans = None
        out_specs = [row, row]
        out_shape = [jax.ShapeDtypeStruct((t, d), f32), jax.ShapeDtypeStruct((t, d), bf16)]
    else:
        out_rows = out_rows or [t]
        assert sum(out_rows) == t
        out_shape = [jax.ShapeDtypeStruct((r, d), f32) for r in out_rows]
        out_spans, _ = _part_tiles(out_shape, tr)
        out_specs = [_part_spec(s, r, tr, d) for s, r in zip(out_spans, out_rows)]
    return pl.pallas_call(
        functools.partial(_resnorm_kernel, res_spans=res_spans, out_spans=out_spans, weight=weight, has_next=has_next),
        grid=(tiles,),
        in_specs=in_specs,
        out_specs=out_specs,
        out_shape=out_shape,
        compiler_params=_params(("arbitrary",), [((tr, d), f32)] * 5, 3 * _nbytes((tr, d), f32)),
        name="resnorm",
    )(*args)


def _groupnorm_kernel(*refs, spans):
    n = len(spans)
    of_refs, om_refs = refs[:n], refs[n:2 * n]
    gf_ref, gm_ref, z_ref = refs[2 * n:]
    fw = of_refs[0].shape[1]
    tr = z_ref.shape[0]
    z_ref[:, :fw] = _rmsnorm(_read_parts(of_refs, spans, tr), gf_ref[...]).astype(z_ref.dtype)
    z_ref[:, fw:] = _rmsnorm(_read_parts(om_refs, spans, tr), gm_ref[...]).astype(z_ref.dtype)


def _groupnorm(of_parts, om_parts, g_f, g_m):
    fw, mw = of_parts[0].shape[1], om_parts[0].shape[1]
    tr = _row_tile(of_parts)
    spans, tiles = _part_tiles(of_parts, tr)
    return pl.pallas_call(
        functools.partial(_groupnorm_kernel, spans=spans),
        grid=(tiles,),
        in_specs=[_part_spec(s, a.shape[0], tr, fw) for s, a in zip(spans, of_parts)]
        + [_part_spec(s, a.shape[0], tr, mw) for s, a in zip(spans, om_parts)]
        + [pl.BlockSpec((1, fw), lambda i: (0, 0)), pl.BlockSpec((1, mw), lambda i: (0, 0))],
        out_specs=pl.BlockSpec((tr, fw + mw), lambda i: (i, 0)),
        out_shape=jax.ShapeDtypeStruct((sum(a.shape[0] for a in of_parts), fw + mw), bf16),
        compiler_params=_params(("arbitrary",), [((tr, fw + mw), f32)] * len(spans) + [((tr, fw + mw), bf16)],
                                2 * _nbytes((tr, fw + mw), f32)),
        name="groupnorm",
    )(*of_parts, *om_parts, g_f.reshape(1, fw), g_m.reshape(1, mw))


def _prefix_lanes(x):
    lane = lax.broadcasted_iota(jnp.int32, x.shape, 1)
    sh = 1
    while sh < x.shape[1]:
        x = x + jnp.where(lane >= sh, pltpu.roll(x, sh, 1), 0.0)
        sh *= 2
    return x


def _cumsum_kernel(lf_ref, c_ref, *, group, inv_scale):
    c = _prefix_lanes(lf_ref[...].T)
    bt = (c * (-inv_scale)).T
    pad = jnp.zeros((bt.shape[0], LANES - 3 * group), f32)
    for kv in range(c_ref.shape[1]):
        b = bt[:, kv * group:(kv + 1) * group]
        hi = b.astype(bf16).astype(f32)
        mid = (b - hi).astype(bf16).astype(f32)
        lo = b - hi - mid
        c_ref[0, kv] = jnp.concatenate([hi, mid, lo, pad], axis=1).astype(bf16)


def _cumsum_logf(logf, *, batch, seq, kvh, scale):
    fh = logf.shape[1]
    g = fh // kvh
    assert 3 * g <= LANES
    return pl.pallas_call(
        functools.partial(_cumsum_kernel, group=g, inv_scale=1.0 / scale),
        grid=(batch,),
        in_specs=[pl.BlockSpec((seq, fh), lambda i: (i, 0))],
        out_specs=pl.BlockSpec((1, kvh, seq, LANES), lambda i: (i, 0, 0, 0)),
        out_shape=jax.ShapeDtypeStruct((batch, kvh, seq, LANES), bf16),
        compiler_params=_params(("arbitrary",), [((seq, LANES), f32), ((kvh * seq, LANES), bf16)],
                                8 * _nbytes((seq, LANES), f32)),
        name="cumsum_logf",
    )(logf)


def _gate_up_kernel(x_ref, wg_ref, wu_ref, o_ref):
    x = x_ref[...]
    g = jnp.dot(x, wg_ref[...].astype(bf16), preferred_element_type=f32)
    u = jnp.dot(x, wu_ref[...].astype(bf16), preferred_element_type=f32)
    o_ref[...] = (jax.nn.silu(g) * u).astype(o_ref.dtype)


def _gate_up(xn, w_gate, w_up):
    t, d = xn.shape
    ff = w_gate.shape[1]
    tm = _divisor_tile(t, MM_ROWS, BF16_ROWS)
    tn = MM_COLS
    return pl.pallas_call(
        _gate_up_kernel,
        grid=(t // tm, pl.cdiv(ff, tn)),
        in_specs=[pl.BlockSpec((tm, d), lambda i, j: (i, 0), pipeline_mode=pl.Buffered(1)),
                  pl.BlockSpec((d, tn), lambda i, j: (0, j)),
                  pl.BlockSpec((d, tn), lambda i, j: (0, j))],
        out_specs=pl.BlockSpec((tm, tn), lambda i, j: (i, j)),
        out_shape=jax.ShapeDtypeStruct((t, ff), bf16),
        compiler_params=_params(("arbitrary", "arbitrary"),
                                [((d, tn), f32), ((d, tn), f32), ((tm, tn), bf16)],
                                _nbytes((tm, d), bf16) + 2 * _nbytes((d, tn), bf16) + 4 * _nbytes((tm, tn), f32)),
        name="gate_up",
    )(xn, w_gate, w_up)


def _matmul_kernel(x_ref, w_ref, o_ref):
    o_ref[...] = jnp.dot(x_ref[...], w_ref[...].astype(bf16), preferred_element_type=f32)


def _matmul(x, w, tn):
    t, k = x.shape
    n = w.shape[1]
    tm = _divisor_tile(t, MM_ROWS if k * tn <= 4096 * MM_COLS_WIDE else MM_ROWS_WIDE_K, BF16_ROWS)
    return pl.pallas_call(
        _matmul_kernel,
        grid=(t // tm, pl.cdiv(n, tn)),
        in_specs=[pl.BlockSpec((tm, k), lambda i, j: (i, 0), pipeline_mode=pl.Buffered(1)),
                  pl.BlockSpec((k, tn), lambda i, j: (0, j))],
        out_specs=pl.BlockSpec((tm, tn), lambda i, j: (i, j)),
        out_shape=jax.ShapeDtypeStruct((t, n), f32),
        compiler_params=_params(("arbitrary", "arbitrary"),
                                [((k, tn), f32), ((tm, tn), f32)],
                                _nbytes((tm, k), bf16) + _nbytes((k, tn), bf16) + 2 * _nbytes((tm, tn), f32)),
        name="matmul",
    )(x, w)


def _proj_kernel(x_ref, wt_ref, cos_ref, sin_ref, b_ref, *out_refs, bounds, hd, fh, kvh):
    j = pl.program_id(1)

    def product(cols=None):
        w = wt_ref[...] if cols is None else wt_ref[:cols, :]
        return lax.dot_general(x_ref[...], w.astype(bf16), _NT, preferred_element_type=f32)

    def rotary(x):
        return x * cos_ref[...] + pltpu.roll(x, hd // 2, 1) * sin_ref[...]

    def query(ref, roped):
        acc = product()
        for c in range(acc.shape[1] // hd):
            x = acc[:, c * hd:(c + 1) * hd]
            ref[:, c * hd:(c + 1) * hd] = (rotary(x) if roped else x).astype(ref.dtype)

    def keyval(ref, head0, roped):
        acc = product()
        for c in range(acc.shape[1] // hd):
            x = acc[:, c * hd:(c + 1) * hd]
            ref[pl.ds(head0 + c, acc.shape[0], stride=kvh), :] = rotary(x) if roped else x

    def forget(ref):
        ref[...] = jax.nn.log_sigmoid(product(LANES)[:, :fh] + b_ref[...])

    qf_ref, kf_ref, vf_ref, qm_ref, km_ref, vm_ref, lf_ref = out_refs
    heads_per_tile = wt_ref.shape[0] // hd
    branches = [(bounds[0], functools.partial(query, qf_ref, False)),
                (bounds[3], functools.partial(query, qm_ref, True)),
                (bounds[6], functools.partial(forget, lf_ref))]
    for sec, ref, roped in ((1, kf_ref, False), (2, vf_ref, False), (4, km_ref, True), (5, vm_ref, False)):
        lo, hi = bounds[sec]
        branches += [((lo + k, lo + k + 1), functools.partial(keyval, ref, k * heads_per_tile, roped))
                     for k in range(hi - lo)]
    for (lo, hi), act in branches:
        @pl.when((j >= lo) & (j < hi))
        def _(act=act):
            act()


def _proj(xn, w_in_t, cosf, sinf, b_forget, *, fw, kvw, mw, fh, hd):
    t, d = xn.shape
    kvh = kvw // hd
    tn = math.gcd(MM_COLS, fw, kvw, mw)
    tm = _divisor_tile(t, PROJ_ROWS, BF16_ROWS)
    widths = [fw, kvw, kvw, mw, kvw, kvw]
    bounds, lo = [], 0
    for w in widths:
        bounds.append((lo, lo + w // tn))
        lo += w // tn
    bounds.append((lo, lo + 1))
    assert lo * tn + fh == w_in_t.shape[0] and fh <= LANES <= tn

    def q_spec(lo, hi):
        return pl.BlockSpec((tm, tn), lambda i, j: (i, jnp.clip(j - lo, 0, hi - lo - 1)))

    kv_spec = pl.BlockSpec((tm * kvh, hd), lambda i, j: (i, 0))
    kv_shape = jax.ShapeDtypeStruct((t * kvh, hd), f32)
    out_specs = [q_spec(*bounds[0]), kv_spec, kv_spec, q_spec(*bounds[3]), kv_spec, kv_spec,
                 pl.BlockSpec((tm, fh), lambda i, j: (i, 0))]
    out_shape = [jax.ShapeDtypeStruct((t, fw), bf16), kv_shape, kv_shape, jax.ShapeDtypeStruct((t, mw), bf16),
                 kv_shape, kv_shape, jax.ShapeDtypeStruct((t, fh), f32)]
    return pl.pallas_call(
        functools.partial(_proj_kernel, bounds=bounds, hd=hd, fh=fh, kvh=kvh),
        grid=(t // tm, bounds[-1][1]),
        in_specs=[pl.BlockSpec((tm, d), lambda i, j: (i, 0), pipeline_mode=pl.Buffered(1)),
                  pl.BlockSpec((tn, d), lambda i, j: (j, 0)),
                  pl.BlockSpec((tm, hd), lambda i, j: (i, 0)),
                  pl.BlockSpec((tm, hd), lambda i, j: (i, 0)),
                  pl.BlockSpec((1, fh), lambda i, j: (0, 0))],
        out_specs=out_specs,
        out_shape=out_shape,
        compiler_params=_params(("arbitrary", "arbitrary"),
                                [((tn, d), f32)] + [((tm, kvw), f32)] * 4 + [((tm, tn), f32)] * 2
                                + [((tm, hd), f32)] * 2,
                                _nbytes((tm, d), bf16) + _nbytes((tn, d), bf16) + 3 * _nbytes((tm, tn), f32)),
        name="proj",
    )(xn, w_in_t, cosf, sinf, b_forget.reshape(1, fh))


MAX_MOBA_BLOCKS = LANES // 2


def _rank_select(gate, n_valid, nblk, axis):
    idx = lax.broadcasted_iota(jnp.int32, gate.shape, axis)
    valid = idx < n_valid
    g = jnp.where(valid, gate, -jnp.inf)
    rank = jnp.zeros(gate.shape, f32)
    size = gate.shape[axis]
    assert size >= 2 * nblk
    for d in range(1, nblk):
        lower = pltpu.roll(g, d, axis)
        upper = pltpu.roll(g, size - d, axis)
        rank = rank + (lower >= g).astype(f32) + (upper > g).astype(f32)
    return jnp.where(valid & (rank < MOBA_TOPK), 0.0, -jnp.inf)


def _attn_kernel(*refs, kind, group, kvh, tq, nblk, scale):
    if kind == "fox":
        q_ref, k_ref, v_ref, c_ref, o_ref, kb_ref, vt_ref, qa_ref, m_ref, acc_ref, s_ref = refs
    else:
        q_ref, k_ref, v_ref, o_ref, kb_ref, vt_ref, qa_ref, m_ref, acc_ref, s_ref, km_ref, sel_ref = refs
    kv_head = pl.program_id(1)
    qi = pl.program_id(2)
    hd = k_ref.shape[-1]
    rows = group * tq

    def block_rows(ref, n):
        return ref[pl.ds(n * tq * kvh + kv_head, tq, stride=kvh), :]

    @pl.when(qi == 0)
    def _():
        means = []
        for n in range(nblk):
            kn = block_rows(k_ref, n)
            kb_ref[n, :, :hd] = kn.astype(bf16)
            if kind == "fox":
                kb_ref[n, :, hd:] = c_ref[0, 0, n * tq:(n + 1) * tq, :]
            else:
                means.append(jnp.mean(kn, axis=0, keepdims=True))
            vt_ref[n, :hd, :] = block_rows(v_ref, n).T.astype(bf16)
            vt_ref[n, hd:, :] = jnp.ones((BF16_ROWS, tq), bf16)
        if kind == "moba":
            km_ref[...] = jnp.concatenate(means + [jnp.zeros((LANES - nblk, hd), f32)], axis=0).astype(bf16)

    q = q_ref[...]
    for g in range(group):
        qa_ref[g * tq:(g + 1) * tq, :hd] = q[:, g * hd:(g + 1) * hd]
    lane_q = lax.broadcasted_iota(jnp.int32, (tq, rows), 1)
    if kind == "fox":
        lane = lax.broadcasted_iota(jnp.int32, (rows, hd), 1)
        head = lax.broadcasted_iota(jnp.int32, (rows, hd), 0) // tq
        pick = (lane == head) | (lane == head + group) | (lane == head + 2 * group)
        qa_ref[:, hd:] = jnp.where(pick, 1.0, 0.0).astype(bf16)
    else:
        gate = lax.dot_general(km_ref[...], qa_ref[...], _NT, preferred_element_type=f32)
        sel_ref[...] = _rank_select(gate[:sel_ref.shape[0]], qi, nblk, 0)

    m_ref[...] = jnp.full(m_ref.shape, -jnp.inf, f32)
    acc_ref[...] = jnp.zeros(acc_ref.shape, f32)

    def scores(j, diagonal):
        s = lax.dot_general(kb_ref[j], qa_ref[...], _NT, preferred_element_type=f32) * scale
        if diagonal:
            key = lax.broadcasted_iota(jnp.int32, s.shape, 0)
            s = jnp.where(key <= (lane_q & (tq - 1)), s, -jnp.inf)
        elif kind == "moba":
            s = s + sel_ref[pl.ds(j, 1), :]
        return s

    def absorb(s, j):
        m_prev = m_ref[...]
        m_new = jnp.maximum(m_prev, jnp.max(s, axis=0, keepdims=True))
        alpha = jnp.exp(m_prev - m_new)
        p = jnp.exp(s - m_new).astype(bf16)
        acc_ref[...] = alpha * acc_ref[...] + jnp.dot(vt_ref[j], p, preferred_element_type=f32)
        m_ref[...] = m_new

    def prev_block(t):
        return jnp.where(t == 0, qi, t - 1)

    s_ref[0] = scores(qi, True)

    def pair(i, carry):
        t = 2 * i
        s_ref[1] = scores(t, False)
        absorb(s_ref[0], prev_block(t))
        s_ref[0] = scores(t + 1, False)
        absorb(s_ref[1], t)
        return carry

    lax.fori_loop(0, lax.shift_right_logical(qi, 1), pair, 0)

    @pl.when((qi & 1) == 1)
    def _():
        t = qi - 1
        s_ref[1] = scores(t, False)
        absorb(s_ref[0], prev_block(t))
        absorb(s_ref[1], t)

    @pl.when((qi & 1) == 0)
    def _():
        absorb(s_ref[0], prev_block(qi))

    acc = acc_ref[...]
    o = (acc[:hd] / acc[hd:hd + 1]).T
    for g in range(group):
        o_ref[:, g * hd:(g + 1) * hd] = o[g * tq:(g + 1) * tq, :]


def _prompt_attention(kind, q, k, v, caug, *, batch, seq, kvh, hd):
    qw = q.shape[1]
    group = qw // (kvh * hd)
    tq = MOBA_BLOCK
    assert tq & (tq - 1) == 0
    nq = seq // tq
    rows = group * tq
    scale = hd ** -0.5
    width = 2 * hd if kind == "fox" else hd
    q_spec = pl.BlockSpec((tq, group * hd), lambda b, h, i: (b * nq + i, h))
    kv_spec = pl.BlockSpec((seq * kvh, hd), lambda b, h, i: (b, 0))
    scratch = [pltpu.VMEM((nq, tq, width), bf16), pltpu.VMEM((nq, hd + BF16_ROWS, tq), bf16),
               pltpu.VMEM((rows, width), bf16), pltpu.VMEM((1, rows), f32), pltpu.VMEM((hd + BF16_ROWS, rows), f32),
               pltpu.VMEM((2, tq, rows), f32)]
    in_specs = [q_spec, kv_spec, kv_spec]
    args = [q, k, v]
    if kind == "fox":
        in_specs.append(pl.BlockSpec((1, 1, seq, LANES), lambda b, h, i: (b, h, 0, 0)))
        args.append(caug)
    else:
        assert nq <= MAX_MOBA_BLOCKS
        sel_rows = -(-2 * nq // SUBLANES) * SUBLANES
        scratch += [pltpu.VMEM((LANES, hd), bf16), pltpu.VMEM((sel_rows, rows), f32)]
    return pl.pallas_call(
        functools.partial(_attn_kernel, kind=kind, group=group, kvh=kvh, tq=tq, nblk=nq, scale=scale),
        grid=(batch, kvh, nq),
        in_specs=in_specs,
        out_specs=q_spec,
        out_shape=jax.ShapeDtypeStruct((batch * seq, qw), f32),
        scratch_shapes=scratch,
        compiler_params=_params(("arbitrary", "arbitrary", "arbitrary"),
                                [((tq, group * hd), bf16), ((seq * kvh, hd), f32), ((seq * kvh, hd), f32),
                                 ((tq, group * hd), f32), ((seq, LANES), bf16)],
                                _nbytes((seq, 3 * hd), bf16) + 8 * _nbytes((tq, rows), f32)),
        name=kind + "_attention",
    )(*args)


def _block_diag_q(q, kvh):
    heads, hd = q.shape
    group = heads // kvh
    qt = jnp.concatenate([q] * kvh, axis=1)
    row_kv = lax.broadcasted_iota(jnp.int32, qt.shape, 0) // group
    col_kv = lax.broadcasted_iota(jnp.int32, qt.shape, 1) // hd
    return jnp.where(row_kv == col_kv, qt, jnp.zeros_like(qt))


def _own_kv_slab(x, kvh):
    heads, w = x.shape
    hd = w // kvh
    group = heads // kvh
    row_kv = lax.broadcasted_iota(jnp.int32, (heads, hd), 0) // group
    out = jnp.zeros((heads, hd), x.dtype)
    for g in range(kvh):
        out = out + jnp.where(row_kv == g, x[:, g * hd:(g + 1) * hd], 0.0)
    return out


def _page_rows(ref, kvh):
    page = ref.shape[1] // kvh
    return jnp.concatenate([ref[0, pl.ds(h, page, stride=kvh), :] for h in range(kvh)], axis=1)


def _decode_kernel(pt_ref, q_ref, *refs, kind, npages, kvh, ppb, scale):
    k_refs = refs[:npages]
    v_refs = refs[npages:2 * npages]
    if kind == "fox":
        lf_refs = refs[2 * npages:3 * npages]
        kn_ref, vn_ref, lfn_ref, o_ref = refs[3 * npages:]
    else:
        kn_ref, vn_ref, o_ref = refs[2 * npages:]
    page = k_refs[0].shape[1] // kvh
    qbd = _block_diag_q(q_ref[0], kvh)
    kn = kn_ref[0].astype(bf16).astype(f32)
    vn = vn_ref[0].astype(bf16).astype(f32)
    s_new = jnp.sum(qbd.astype(f32) * kn, axis=1, keepdims=True) * scale

    k_pages = [_page_rows(r, kvh) for r in k_refs]
    s = jnp.concatenate([lax.dot_general(qbd, kp.astype(bf16), _NT, preferred_element_type=f32) for kp in k_pages],
                        axis=1) * scale
    if kind == "fox":
        c_in = _prefix_lanes(jnp.concatenate([r[0] for r in lf_refs], axis=1))
        s = s - c_in
        s_new = s_new - (c_in[:, c_in.shape[1] - 1:] + lfn_ref[0])
    else:
        nblk = npages // ppb
        sums = [sum(jnp.sum(k_pages[n * ppb + i], axis=0, keepdims=True) for i in range(ppb)) for n in range(nblk)]
        kmean = jnp.concatenate(sums + [jnp.zeros((LANES - nblk, qbd.shape[1]), f32)], axis=0) / (ppb * page)
        gate = lax.dot_general(qbd, kmean.astype(bf16), _NT, preferred_element_type=f32)
        sel = _rank_select(gate, nblk, nblk, 1)
        s = s + jnp.concatenate([jnp.broadcast_to(sel[:, n:n + 1], (sel.shape[0], ppb * page)) for n in range(nblk)],
                                axis=1)
    m = jnp.maximum(jnp.max(s, axis=1, keepdims=True), s_new)
    p = jnp.exp(s - m)
    p_new = jnp.exp(s_new - m)
    l = jnp.sum(p, axis=1, keepdims=True) + p_new
    acc = p_new * vn
    for i in range(npages):
        acc = acc + jnp.dot(p[:, i * page:(i + 1) * page].astype(bf16), _page_rows(v_refs[i], kvh).astype(bf16),
                            preferred_element_type=f32)
    o_ref[0] = _own_kv_slab(acc / l, kvh)


def _decode_attention(kind, q, k_pool, v_pool, lf_pool, k_new, v_new, lf_new, page_table, *, kvh, hd):
    db, npages = page_table.shape
    heads = q.shape[1] // hd
    kvw = kvh * hd
    page = k_pool.shape[1]
    scale = hd ** -0.5
    ppb = MOBA_BLOCK // page
    assert npages % ppb == 0 and 2 * (npages // ppb) <= LANES
    q3 = q.reshape(db, heads, hd)
    k_flat = k_pool.reshape(k_pool.shape[0], page * kvh, hd)
    v_flat = v_pool.reshape(v_pool.shape[0], page * kvh, hd)

    def paged_kv(i):
        return pl.BlockSpec((1, page * kvh, hd), lambda b, pt: (pt[b * npages + i], 0, 0))

    per_seq = lambda shape: pl.BlockSpec((1,) + shape, lambda b, pt: (b, 0, 0))
    in_specs = [per_seq((heads, hd))] + [paged_kv(i) for i in range(npages)] * 2
    args = [q3] + [k_flat] * npages + [v_flat] * npages
    if kind == "fox":
        lf_t = jnp.swapaxes(lf_pool, 1, 2)
        in_specs += [pl.BlockSpec((1, heads, page), lambda b, pt, i=i: (pt[b * npages + i], 0, 0))
                     for i in range(npages)]
        args += [lf_t] * npages
        in_specs += [per_seq((1, kvw)), per_seq((1, kvw)), per_seq((heads, 1))]
        args += [k_new.reshape(db, 1, kvw), v_new.reshape(db, 1, kvw), lf_new.reshape(db, heads, 1)]
    else:
        in_specs += [per_seq((1, kvw)), per_seq((1, kvw))]
        args += [k_new.reshape(db, 1, kvw), v_new.reshape(db, 1, kvw)]
    out = pl.pallas_call(
        functools.partial(_decode_kernel, kind=kind, npages=npages, kvh=kvh, ppb=ppb, scale=scale),
        grid_spec=pltpu.PrefetchScalarGridSpec(
            num_scalar_prefetch=1,
            grid=(db,),
            in_specs=in_specs,
            out_specs=pl.BlockSpec((1, heads, hd), lambda b, pt: (b, 0, 0)),
        ),
        out_shape=jax.ShapeDtypeStruct((db, heads, hd), f32),
        compiler_params=_params(("arbitrary",), [((page, kvw), f32)] * (2 * npages),
                                6 * npages * _nbytes((page, kvw), bf16)),
        name=kind + "_decode",
    )(page_table.reshape(-1), *args)
    return out.reshape(db, heads * hd)


def _rope_tables(pos, hd):
    half = hd // 2
    inv_freq = ROPE_THETA ** (-jnp.arange(half, dtype=f32) / half)
    ang = pos.astype(f32)[:, None] * inv_freq[None, :]
    cos, sin = jnp.cos(ang), jnp.sin(ang)
    return jnp.concatenate([cos, cos], axis=-1), jnp.concatenate([-sin, sin], axis=-1)


def kernel(x_prompt, x_sample, cache_fox_k, cache_fox_v, cache_fox_logf, cache_moba_k, cache_moba_v, page_table,
           g_pre_ffn1, g_post_ffn1, w1_gate, w1_up, w1_down, g_pre_mix, w_in, b_forget, g_out_fox, g_out_moba,
           w_out, g_post_mix, g_pre_ffn2, g_post_ffn2, w2_gate, w2_up, w2_down):
    depth = w_in.shape[0]
    assert depth == 1, "one layer: the sample group reads the caches the prompt group does not write"
    batch, seq, d = x_prompt.shape
    db, dseq, _ = x_sample.shape
    assert dseq == 1
    kvh, hd = cache_fox_k.shape[3], cache_fox_k.shape[4]
    assert hd == LANES
    kvw = kvh * hd
    fh = b_forget.shape[-1]
    fw = fh * hd
    mw = w_in.shape[-1] - fw - 4 * kvw - fh
    page = cache_fox_k.shape[2]
    past_len = page_table.shape[1] * page
    assert seq % MOBA_BLOCK == 0 and past_len % MOBA_BLOCK == 0
    tp = batch * seq

    def layer0(a):
        return a.reshape(a.shape[1:])

    def half_ffn(res_parts, hn, g_post, w_gate, w_up, w_down, **kw):
        a = _gate_up(hn, layer0(w_gate), layer0(w_up))
        y = _matmul(a, layer0(w_down), MM_COLS)
        return _resnorm(y, res_parts, layer0(g_post), 0.5, **kw)

    x_parts = [x_prompt.reshape(tp, d), x_sample.reshape(db, d)]
    hn = _prenorm(x_parts, layer0(g_pre_ffn1))
    h, hn = half_ffn(x_parts, hn, g_post_ffn1, w1_gate, w1_up, w1_down, g_next=layer0(g_pre_mix))

    pos = jnp.concatenate([jnp.tile(jnp.arange(seq, dtype=jnp.int32), batch), jnp.full((db,), past_len, jnp.int32)])
    cosf, sinf = _rope_tables(pos, hd)
    w_in_t = jnp.swapaxes(layer0(w_in), 0, 1)
    qf, kf, vf, qm, km, vm, lf = _proj(hn, w_in_t, cosf, sinf, layer0(b_forget), fw=fw, kvw=kvw, mw=mw, fh=fh, hd=hd)

    scale = hd ** -0.5
    caug = _cumsum_logf(lf, batch=batch, seq=seq, kvh=kvh, scale=scale)
    of_p = _prompt_attention("fox", qf, kf, vf, caug, batch=batch, seq=seq, kvh=kvh, hd=hd)
    om_p = _prompt_attention("moba", qm, km, vm, None, batch=batch, seq=seq, kvh=kvh, hd=hd)
    of_s = _decode_attention("fox", qf[tp:], layer0(cache_fox_k), layer0(cache_fox_v), layer0(cache_fox_logf),
                             kf[tp * kvh:], vf[tp * kvh:], lf[tp:], page_table, kvh=kvh, hd=hd)
    om_s = _decode_attention("moba", qm[tp:], layer0(cache_moba_k), layer0(cache_moba_v), None,
                             km[tp * kvh:], vm[tp * kvh:], None, page_table, kvh=kvh, hd=hd)

    z = _groupnorm([of_p, of_s], [om_p, om_s], layer0(g_out_fox), layer0(g_out_moba))
    y = _matmul(z, layer0(w_out), MM_COLS_WIDE)
    h, hn = _resnorm(y, [h], layer0(g_post_mix), 1.0, g_next=layer0(g_pre_ffn2))
    yp, ys = half_ffn([h], hn, g_post_ffn2, w2_gate, w2_up, w2_down, out_rows=[tp, db])

    def rows(a, lo, b, s):
        return a[lo * kvh:(lo + b * s) * kvh].reshape(depth, b, s, kvh, hd)

    return (yp.reshape(batch, seq, d), ys.reshape(db, dseq, d),
            rows(kf, 0, batch, seq), rows(vf, 0, batch, seq), lf[:tp].reshape(depth, batch, seq, fh),
            rows(km, 0, batch, seq), rows(vm, 0, batch, seq),
            rows(kf, tp, db, dseq), rows(vf, tp, db, dseq), lf[tp:].reshape(depth, db, dseq, fh),
            rows(km, tp, db, dseq), rows(vm, tp, db, dseq))
```

```python
import functools
import math

import jax
import jax.numpy as jnp
from jax import lax
from jax.experimental import pallas as pl
from jax.experimental.pallas import tpu as pltpu

f32 = jnp.float32
bf16 = jnp.bfloat16

MOBA_BLOCK = 256
MOBA_TOPK = 3
ROPE_THETA = 10000.0
RMS_EPS = 1e-6
LANES = 128
SUBLANES = 8
BF16_ROWS = 16
ROW_TILE = 256
MM_ROWS = 1664
MM_ROWS_WIDE_K = 1040
PROJ_ROWS = 1040
MM_COLS = 256
MM_COLS_WIDE = 512
VMEM_CAP = 60 * 1024 * 1024

_NT = (((1,), (1,)), ((), ()))


def _nbytes(shape, dtype):
    n = 1
    for s in shape:
        n *= s
    return n * jnp.dtype(dtype).itemsize


def _params(sem, pipelined, resident=0):
    est = 2 * sum(_nbytes(s, d) for s, d in pipelined) + resident
    limit = min(VMEM_CAP, max(32 * 1024 * 1024, est + est // 4))
    return pltpu.CompilerParams(dimension_semantics=sem, vmem_limit_bytes=int(limit))


def _divisor_tile(t, cap, mult):
    best = None
    for c in range(mult, min(t, cap) + 1, mult):
        if t % c == 0:
            best = c
    assert best is not None, (t, cap, mult)
    return best


def _rmsnorm(x, g):
    ms = jnp.mean(x * x, axis=-1, keepdims=True)
    return x * lax.rsqrt(ms + RMS_EPS) * g


def _part_tiles(parts, tr):
    spans, start = [], 0
    for n, a in enumerate(parts):
        assert a.shape[0] % tr == 0 or n == len(parts) - 1
        count = pl.cdiv(a.shape[0], tr)
        spans.append((start, count))
        start += count
    return spans, start


def _part_spec(span, rows, tr, width):
    first, count = span
    return pl.BlockSpec((min(tr, rows), width), lambda i: (jnp.clip(i - first, 0, count - 1), 0))


def _read_parts(refs, spans, tr):
    i = pl.program_id(0)
    x = None
    for r, (first, _) in zip(refs, spans):
        v = r[...]
        if v.shape[0] < tr:
            v = jnp.concatenate([v, jnp.zeros((tr - v.shape[0], v.shape[1]), v.dtype)], axis=0)
        x = v if x is None else jnp.where(i >= first, v, x)
    return x


def _write_parts(refs, spans, x):
    i = pl.program_id(0)
    for r, (first, count) in zip(refs, spans):
        @pl.when((i >= first) & (i < first + count))
        def _(r=r):
            r[...] = x[:r.shape[0]].astype(r.dtype)


def _row_tile(parts):
    return min(ROW_TILE, sum(a.shape[0] for a in parts))


def _prenorm_kernel(*refs, spans):
    n = len(spans)
    g_ref, o_ref = refs[n], refs[n + 1]
    o_ref[...] = _rmsnorm(_read_parts(refs[:n], spans, o_ref.shape[0]), g_ref[...]).astype(o_ref.dtype)


def _prenorm(x_parts, g):
    d = x_parts[0].shape[1]
    tr = _row_tile(x_parts)
    spans, tiles = _part_tiles(x_parts, tr)
    return pl.pallas_call(
        functools.partial(_prenorm_kernel, spans=spans),
        grid=(tiles,),
        in_specs=[_part_spec(s, a.shape[0], tr, d) for s, a in zip(spans, x_parts)]
        + [pl.BlockSpec((1, d), lambda i: (0, 0))],
        out_specs=pl.BlockSpec((tr, d), lambda i: (i, 0)),
        out_shape=jax.ShapeDtypeStruct((sum(a.shape[0] for a in x_parts), d), bf16),
        compiler_params=_params(("arbitrary",), [((tr, d), f32)] * len(spans) + [((tr, d), bf16)],
                                2 * _nbytes((tr, d), f32)),
        name="prenorm",
    )(*x_parts, g.reshape(1, d))


def _resnorm_kernel(*refs, res_spans, out_spans, weight, has_next):
    y_ref = refs[0]
    nres = len(res_spans)
    res_refs = refs[1:1 + nres]
    gp_ref = refs[1 + nres]
    rest = refs[2 + nres:]
    y = _rmsnorm(y_ref[...], gp_ref[...])
    if weight != 1.0:
        y = weight * y
    h = _read_parts(res_refs, res_spans, y.shape[0]) + y
    if has_next:
        gn_ref, h_ref, hn_ref = rest
        h_ref[...] = h
        hn_ref[...] = _rmsnorm(h, gn_ref[...]).astype(hn_ref.dtype)
    else:
        _write_parts(rest, out_spans, h)


def _resnorm(y, res_parts, g_post, weight, g_next=None, out_rows=None):
    t, d = y.shape
    tr = _row_tile(res_parts)
    res_spans, tiles = _part_tiles(res_parts, tr)
    assert sum(a.shape[0] for a in res_parts) == t
    row = pl.BlockSpec((tr, d), lambda i: (i, 0))
    vec = pl.BlockSpec((1, d), lambda i: (0, 0))
    has_next = g_next is not None
    in_specs = ([row] + [_part_spec(s, a.shape[0], tr, d) for s, a in zip(res_spans, res_parts)] + [vec]
                + ([vec] if has_next else []))
    args = [y, *res_parts, g_post.reshape(1, d)] + ([g_next.reshape(1, d)] if has_next else [])
    if has_next:
        out_spans = None
        out_specs = [row, row]
        out_shape = [jax.ShapeDtypeStruct((t, d), f32), jax.ShapeDtypeStruct((t, d), bf16)]
    else:
        out_rows = out_rows or [t]
        assert sum(out_rows) == t
        out_shape = [jax.ShapeDtypeStruct((r, d), f32) for r in out_rows]
        out_spans, _ = _part_tiles(out_shape, tr)
        out_specs = [_part_spec(s, r, tr, d) for s, r in zip(out_spans, out_rows)]
    return pl.pallas_call(
        functools.partial(_resnorm_kernel, res_spans=res_spans, out_spans=out_spans, weight=weight, has_next=has_next),
        grid=(tiles,),
        in_specs=in_specs,
        out_specs=out_specs,
        out_shape=out_shape,
        compiler_params=_params(("arbitrary",), [((tr, d), f32)] * 5, 3 * _nbytes((tr, d), f32)),
        name="resnorm",
    )(*args)


def _groupnorm_kernel(*refs, spans):
    n = len(spans)
    of_refs, om_refs = refs[:n], refs[n:2 * n]
    gf_ref, gm_ref, z_ref = refs[2 * n:]
    fw = of_refs[0].shape[1]
    tr = z_ref.shape[0]
    z_ref[:, :fw] = _rmsnorm(_read_parts(of_refs, spans, tr), gf_ref[...]).astype(z_ref.dtype)
    z_ref[:, fw:] = _rmsnorm(_read_parts(om_refs, spans, tr), gm_ref[...]).astype(z_ref.dtype)


def _groupnorm(of_parts, om_parts, g_f, g_m):
    fw, mw = of_parts[0].shape[1], om_parts[0].shape[1]
    tr = _row_tile(of_parts)
    spans, tiles = _part_tiles(of_parts, tr)
    return pl.pallas_call(
        functools.partial(_groupnorm_kernel, spans=spans),
        grid=(tiles,),
        in_specs=[_part_spec(s, a.shape[0], tr, fw) for s, a in zip(spans, of_parts)]
        + [_part_spec(s, a.shape[0], tr, mw) for s, a in zip(spans, om_parts)]
        + [pl.BlockSpec((1, fw), lambda i: (0, 0)), pl.BlockSpec((1, mw), lambda i: (0, 0))],
        out_specs=pl.BlockSpec((tr, fw + mw), lambda i: (i, 0)),
        out_shape=jax.ShapeDtypeStruct((sum(a.shape[0] for a in of_parts), fw + mw), bf16),
        compiler_params=_params(("arbitrary",), [((tr, fw + mw), f32)] * len(spans) + [((tr, fw + mw), bf16)],
                                2 * _nbytes((tr, fw + mw), f32)),
        name="groupnorm",
    )(*of_parts, *om_parts, g_f.reshape(1, fw), g_m.reshape(1, mw))


def _prefix_lanes(x):
    lane = lax.broadcasted_iota(jnp.int32, x.shape, 1)
    sh = 1
    while sh < x.shape[1]:
        x = x + jnp.where(lane >= sh, pltpu.roll(x, sh, 1), 0.0)
        sh *= 2
    return x


def _cumsum_kernel(lf_ref, c_ref, *, group, inv_scale):
    c = _prefix_lanes(lf_ref[...].T)
    bt = (c * (-inv_scale)).T
    pad = jnp.zeros((bt.shape[0], LANES - 3 * group), f32)
    for kv in range(c_ref.shape[1]):
        b = bt[:, kv * group:(kv + 1) * group]
        hi = b.astype(bf16).astype(f32)
        mid = (b - hi).astype(bf16).astype(f32)
        lo = b - hi - mid
        c_ref[0, kv] = jnp.concatenate([hi, mid, lo, pad], axis=1).astype(bf16)


def _cumsum_logf(logf, *, batch, seq, kvh, scale):
    fh = logf.shape[1]
    g = fh // kvh
    assert 3 * g <= LANES
    return pl.pallas_call(
        functools.partial(_cumsum_kernel, group=g, inv_scale=1.0 / scale),
        grid=(batch,),
        in_specs=[pl.BlockSpec((seq, fh), lambda i: (i, 0))],
        out_specs=pl.BlockSpec((1, kvh, seq, LANES), lambda i: (i, 0, 0, 0)),
        out_shape=jax.ShapeDtypeStruct((batch, kvh, seq, LANES), bf16),
        compiler_params=_params(("arbitrary",), [((seq, LANES), f32), ((kvh * seq, LANES), bf16)],
                                8 * _nbytes((seq, LANES), f32)),
        name="cumsum_logf",
    )(logf)


def _gate_up_kernel(x_ref, wg_ref, wu_ref, o_ref):
    x = x_ref[...]
    g = jnp.dot(x, wg_ref[...].astype(bf16), preferred_element_type=f32)
    u = jnp.dot(x, wu_ref[...].astype(bf16), preferred_element_type=f32)
    o_ref[...] = (jax.nn.silu(g) * u).astype(o_ref.dtype)


def _gate_up(xn, w_gate, w_up):
    t, d = xn.shape
    ff = w_gate.shape[1]
    tm = _divisor_tile(t, MM_ROWS, BF16_ROWS)
    tn = MM_COLS
    return pl.pallas_call(
        _gate_up_kernel,
        grid=(t // tm, pl.cdiv(ff, tn)),
        in_specs=[pl.BlockSpec((tm, d), lambda i, j: (i, 0)),
                  pl.BlockSpec((d, tn), lambda i, j: (0, j)),
                  pl.BlockSpec((d, tn), lambda i, j: (0, j))],
        out_specs=pl.BlockSpec((tm, tn), lambda i, j: (i, j)),
        out_shape=jax.ShapeDtypeStruct((t, ff), bf16),
        compiler_params=_params(("arbitrary", "arbitrary"),
                                [((tm, d), bf16), ((d, tn), f32), ((d, tn), f32), ((tm, tn), bf16)],
                                2 * _nbytes((d, tn), bf16) + 4 * _nbytes((tm, tn), f32)),
        name="gate_up",
    )(xn, w_gate, w_up)


def _matmul_kernel(x_ref, w_ref, o_ref):
    o_ref[...] = jnp.dot(x_ref[...], w_ref[...].astype(bf16), preferred_element_type=f32)


def _matmul(x, w, tn):
    t, k = x.shape
    n = w.shape[1]
    wide = k * tn > 4096 * MM_COLS_WIDE
    tm = _divisor_tile(t, MM_ROWS_WIDE_K if wide else MM_ROWS, BF16_ROWS)
    lhs_mode = dict(pipeline_mode=pl.Buffered(1)) if wide else {}
    return pl.pallas_call(
        _matmul_kernel,
        grid=(t // tm, pl.cdiv(n, tn)),
        in_specs=[pl.BlockSpec((tm, k), lambda i, j: (i, 0), **lhs_mode),
                  pl.BlockSpec((k, tn), lambda i, j: (0, j))],
        out_specs=pl.BlockSpec((tm, tn), lambda i, j: (i, j)),
        out_shape=jax.ShapeDtypeStruct((t, n), f32),
        compiler_params=_params(("arbitrary", "arbitrary"),
                                [((k, tn), f32), ((tm, tn), f32)],
                                (1 if wide else 2) * _nbytes((tm, k), bf16) + _nbytes((k, tn), bf16)
                                + 2 * _nbytes((tm, tn), f32)),
        name="matmul",
    )(x, w)


def _proj_kernel(x_ref, wt_ref, cos_ref, sin_ref, b_ref, *out_refs, bounds, hd, fh, kvh):
    j = pl.program_id(1)

    def product(cols=None):
        w = wt_ref[...] if cols is None else wt_ref[:cols, :]
        return lax.dot_general(x_ref[...], w.astype(bf16), _NT, preferred_element_type=f32)

    def rotary(x):
        return x * cos_ref[...] + pltpu.roll(x, hd // 2, 1) * sin_ref[...]

    def query(ref, roped):
        acc = product()
        for c in range(acc.shape[1] // hd):
            x = acc[:, c * hd:(c + 1) * hd]
            ref[:, c * hd:(c + 1) * hd] = (rotary(x) if roped else x).astype(ref.dtype)

    def keyval(ref, head0, roped):
        acc = product()
        for c in range(acc.shape[1] // hd):
            x = acc[:, c * hd:(c + 1) * hd]
            ref[pl.ds(head0 + c, acc.shape[0], stride=kvh), :] = rotary(x) if roped else x

    def forget(ref):
        ref[...] = jax.nn.log_sigmoid(product(LANES)[:, :fh] + b_ref[...])

    qf_ref, kf_ref, vf_ref, qm_ref, km_ref, vm_ref, lf_ref = out_refs
    heads_per_tile = wt_ref.shape[0] // hd
    branches = [(bounds[0], functools.partial(query, qf_ref, False)),
                (bounds[3], functools.partial(query, qm_ref, True)),
                (bounds[6], functools.partial(forget, lf_ref))]
    for sec, ref, roped in ((1, kf_ref, False), (2, vf_ref, False), (4, km_ref, True), (5, vm_ref, False)):
        lo, hi = bounds[sec]
        branches += [((lo + k, lo + k + 1), functools.partial(keyval, ref, k * heads_per_tile, roped))
                     for k in range(hi - lo)]
    for (lo, hi), act in branches:
        @pl.when((j >= lo) & (j < hi))
        def _(act=act):
            act()


def _proj(xn, w_in_t, cosf, sinf, b_forget, *, fw, kvw, mw, fh, hd):
    t, d = xn.shape
    kvh = kvw // hd
    tn = math.gcd(MM_COLS_WIDE, fw, kvw, mw)
    tm = _divisor_tile(t, PROJ_ROWS, BF16_ROWS)
    widths = [fw, kvw, kvw, mw, kvw, kvw]
    bounds, lo = [], 0
    for w in widths:
        bounds.append((lo, lo + w // tn))
        lo += w // tn
    bounds.append((lo, lo + 1))
    assert lo * tn + fh == w_in_t.shape[0] and fh <= LANES <= tn

    def q_spec(lo, hi):
        return pl.BlockSpec((tm, tn), lambda i, j: (i, jnp.clip(j - lo, 0, hi - lo - 1)))

    kv_spec = pl.BlockSpec((tm * kvh, hd), lambda i, j: (i, 0))
    kv_shape = jax.ShapeDtypeStruct((t * kvh, hd), f32)
    out_specs = [q_spec(*bounds[0]), kv_spec, kv_spec, q_spec(*bounds[3]), kv_spec, kv_spec,
                 pl.BlockSpec((tm, fh), lambda i, j: (i, 0))]
    out_shape = [jax.ShapeDtypeStruct((t, fw), bf16), kv_shape, kv_shape, jax.ShapeDtypeStruct((t, mw), bf16),
                 kv_shape, kv_shape, jax.ShapeDtypeStruct((t, fh), f32)]
    return pl.pallas_call(
        functools.partial(_proj_kernel, bounds=bounds, hd=hd, fh=fh, kvh=kvh),
        grid=(t // tm, bounds[-1][1]),
        in_specs=[pl.BlockSpec((tm, d), lambda i, j: (i, 0), pipeline_mode=pl.Buffered(1)),
                  pl.BlockSpec((tn, d), lambda i, j: (j, 0)),
                  pl.BlockSpec((tm, hd), lambda i, j: (i, 0)),
                  pl.BlockSpec((tm, hd), lambda i, j: (i, 0)),
                  pl.BlockSpec((1, fh), lambda i, j: (0, 0))],
        out_specs=out_specs,
        out_shape=out_shape,
        compiler_params=_params(("arbitrary", "arbitrary"),
                                [((tn, d), f32)] + [((tm, kvw), f32)] * 4 + [((tm, tn), f32)] * 2
                                + [((tm, hd), f32)] * 2,
                                _nbytes((tm, d), bf16) + _nbytes((tn, d), bf16) + 3 * _nbytes((tm, tn), f32)),
        name="proj",
    )(xn, w_in_t, cosf, sinf, b_forget.reshape(1, fh))


MAX_MOBA_BLOCKS = LANES // 2


def _rank_select(gate, n_valid, nblk, axis):
    idx = lax.broadcasted_iota(jnp.int32, gate.shape, axis)
    valid = idx < n_valid
    g = jnp.where(valid, gate, -jnp.inf)
    rank = jnp.zeros(gate.shape, f32)
    size = gate.shape[axis]
    assert size >= 2 * nblk
    for d in range(1, nblk):
        lower = pltpu.roll(g, d, axis)
        upper = pltpu.roll(g, size - d, axis)
        rank = rank + (lower >= g).astype(f32) + (upper > g).astype(f32)
    return jnp.where(valid & (rank < MOBA_TOPK), 0.0, -jnp.inf)


def _attn_kernel(*refs, kind, group, kvh, tq, nblk, scale):
    if kind == "fox":
        q_ref, k_ref, v_ref, c_ref, o_ref, kb_ref, vt_ref, qa_ref, m_ref, acc_ref, s_ref = refs
    else:
        q_ref, k_ref, v_ref, o_ref, kb_ref, vt_ref, qa_ref, m_ref, acc_ref, s_ref, km_ref, sel_ref = refs
    kv_head = pl.program_id(1)
    qi = pl.program_id(2)
    hd = k_ref.shape[-1]
    rows = group * tq

    def block_rows(ref, n):
        return ref[pl.ds(n * tq * kvh + kv_head, tq, stride=kvh), :]

    @pl.when(qi == 0)
    def _():
        means = []
        for n in range(nblk):
            kn = block_rows(k_ref, n)
            kb_ref[n, :, :hd] = kn.astype(bf16)
            if kind == "fox":
                kb_ref[n, :, hd:] = c_ref[0, 0, n * tq:(n + 1) * tq, :]
            else:
                means.append(jnp.mean(kn, axis=0, keepdims=True))
            vt_ref[n, :hd, :] = block_rows(v_ref, n).T.astype(bf16)
            vt_ref[n, hd:, :] = jnp.ones((BF16_ROWS, tq), bf16)
        if kind == "moba":
            km_ref[...] = jnp.concatenate(means + [jnp.zeros((LANES - nblk, hd), f32)], axis=0).astype(bf16)

    q = q_ref[...]
    for g in range(group):
        qa_ref[g * tq:(g + 1) * tq, :hd] = q[:, g * hd:(g + 1) * hd]
    lane_q = lax.broadcasted_iota(jnp.int32, (tq, rows), 1)
    if kind == "fox":
        lane = lax.broadcasted_iota(jnp.int32, (rows, hd), 1)
        head = lax.broadcasted_iota(jnp.int32, (rows, hd), 0) // tq
        pick = (lane == head) | (lane == head + group) | (lane == head + 2 * group)
        qa_ref[:, hd:] = jnp.where(pick, 1.0, 0.0).astype(bf16)
    else:
        gate = lax.dot_general(km_ref[...], qa_ref[...], _NT, preferred_element_type=f32)
        sel_ref[...] = _rank_select(gate[:sel_ref.shape[0]], qi, nblk, 0)

    m_ref[...] = jnp.full(m_ref.shape, -jnp.inf, f32)
    acc_ref[...] = jnp.zeros(acc_ref.shape, f32)

    def scores(j, diagonal):
        s = lax.dot_general(kb_ref[j], qa_ref[...], _NT, preferred_element_type=f32) * scale
        if diagonal:
            key = lax.broadcasted_iota(jnp.int32, s.shape, 0)
            s = jnp.where(key <= (lane_q & (tq - 1)), s, -jnp.inf)
        elif kind == "moba":
            s = s + sel_ref[pl.ds(j, 1), :]
        return s

    def absorb(s, j):
        m_prev = m_ref[...]
        m_new = jnp.maximum(m_prev, jnp.max(s, axis=0, keepdims=True))
        alpha = jnp.exp(m_prev - m_new)
        p = jnp.exp(s - m_new).astype(bf16)
        acc_ref[...] = alpha * acc_ref[...] + jnp.dot(vt_ref[j], p, preferred_element_type=f32)
        m_ref[...] = m_new

    def prev_block(t):
        return jnp.where(t == 0, qi, t - 1)

    s_ref[0] = scores(qi, True)

    def pair(i, carry):
        t = 2 * i
        s_ref[1] = scores(t, False)
        absorb(s_ref[0], prev_block(t))
        s_ref[0] = scores(t + 1, False)
        absorb(s_ref[1], t)
        return carry

    lax.fori_loop(0, lax.shift_right_logical(qi, 1), pair, 0)

    @pl.when((qi & 1) == 1)
    def _():
        t = qi - 1
        s_ref[1] = scores(t, False)
        absorb(s_ref[0], prev_block(t))
        absorb(s_ref[1], t)

    @pl.when((qi & 1) == 0)
    def _():
        absorb(s_ref[0], prev_block(qi))

    acc = acc_ref[...]
    o = (acc[:hd] / acc[hd:hd + 1]).T
    for g in range(group):
        o_ref[:, g * hd:(g + 1) * hd] = o[g * tq:(g + 1) * tq, :]


def _prompt_attention(kind, q, k, v, caug, *, batch, seq, kvh, hd):
    qw = q.shape[1]
    group = qw // (kvh * hd)
    tq = MOBA_BLOCK
    assert tq & (tq - 1) == 0
    nq = seq // tq
    rows = group * tq
    scale = hd ** -0.5
    width = 2 * hd if kind == "fox" else hd
    q_spec = pl.BlockSpec((tq, group * hd), lambda b, h, i: (b * nq + i, h))
    kv_spec = pl.BlockSpec((seq * kvh, hd), lambda b, h, i: (b, 0))
    scratch = [pltpu.VMEM((nq, tq, width), bf16), pltpu.VMEM((nq, hd + BF16_ROWS, tq), bf16),
               pltpu.VMEM((rows, width), bf16), pltpu.VMEM((1, rows), f32), pltpu.VMEM((hd + BF16_ROWS, rows), f32),
               pltpu.VMEM((2, tq, rows), f32)]
    in_specs = [q_spec, kv_spec, kv_spec]
    args = [q, k, v]
    if kind == "fox":
        in_specs.append(pl.BlockSpec((1, 1, seq, LANES), lambda b, h, i: (b, h, 0, 0)))
        args.append(caug)
    else:
        assert nq <= MAX_MOBA_BLOCKS
        sel_rows = -(-2 * nq // SUBLANES) * SUBLANES
        scratch += [pltpu.VMEM((LANES, hd), bf16), pltpu.VMEM((sel_rows, rows), f32)]
    return pl.pallas_call(
        functools.partial(_attn_kernel, kind=kind, group=group, kvh=kvh, tq=tq, nblk=nq, scale=scale),
        grid=(batch, kvh, nq),
        in_specs=in_specs,
        out_specs=q_spec,
        out_shape=jax.ShapeDtypeStruct((batch * seq, qw), f32),
        scratch_shapes=scratch,
        compiler_params=_params(("arbitrary", "arbitrary", "arbitrary"),
                                [((tq, group * hd), bf16), ((seq * kvh, hd), f32), ((seq * kvh, hd), f32),
                                 ((tq, group * hd), f32), ((seq, LANES), bf16)],
                                _nbytes((seq, 3 * hd), bf16) + 8 * _nbytes((tq, rows), f32)),
        name=kind + "_attention",
    )(*args)


def _block_diag_q(q, kvh):
    heads, hd = q.shape
    group = heads // kvh
    qt = jnp.concatenate([q] * kvh, axis=1)
    row_kv = lax.broadcasted_iota(jnp.int32, qt.shape, 0) // group
    col_kv = lax.broadcasted_iota(jnp.int32, qt.shape, 1) // hd
    return jnp.where(row_kv == col_kv, qt, jnp.zeros_like(qt))


def _own_kv_slab(x, kvh):
    heads, w = x.shape
    hd = w // kvh
    group = heads // kvh
    row_kv = lax.broadcasted_iota(jnp.int32, (heads, hd), 0) // group
    out = jnp.zeros((heads, hd), x.dtype)
    for g in range(kvh):
        out = out + jnp.where(row_kv == g, x[:, g * hd:(g + 1) * hd], 0.0)
    return out


def _page_rows(ref, kvh):
    page = ref.shape[1] // kvh
    return jnp.concatenate([ref[0, pl.ds(h, page, stride=kvh), :] for h in range(kvh)], axis=1)


def _decode_kernel(pt_ref, q_ref, *refs, kind, npages, kvh, ppb, scale):
    k_refs = refs[:npages]
    v_refs = refs[npages:2 * npages]
    if kind == "fox":
        lf_refs = refs[2 * npages:3 * npages]
        kn_ref, vn_ref, lfn_ref, o_ref = refs[3 * npages:]
    else:
        kn_ref, vn_ref, o_ref = refs[2 * npages:]
    page = k_refs[0].shape[1] // kvh
    qbd = _block_diag_q(q_ref[0], kvh)
    kn = kn_ref[0].astype(bf16).astype(f32)
    vn = vn_ref[0].astype(bf16).astype(f32)
    s_new = jnp.sum(qbd.astype(f32) * kn, axis=1, keepdims=True) * scale

    k_pages = [_page_rows(r, kvh) for r in k_refs]
    s = jnp.concatenate([lax.dot_general(qbd, kp.astype(bf16), _NT, preferred_element_type=f32) for kp in k_pages],
                        axis=1) * scale
    if kind == "fox":
        c_in = _prefix_lanes(jnp.concatenate([r[0] for r in lf_refs], axis=1))
        s = s - c_in
        s_new = s_new - (c_in[:, c_in.shape[1] - 1:] + lfn_ref[0])
    else:
        nblk = npages // ppb
        sums = [sum(jnp.sum(k_pages[n * ppb + i], axis=0, keepdims=True) for i in range(ppb)) for n in range(nblk)]
        kmean = jnp.concatenate(sums + [jnp.zeros((LANES - nblk, qbd.shape[1]), f32)], axis=0) / (ppb * page)
        gate = lax.dot_general(qbd, kmean.astype(bf16), _NT, preferred_element_type=f32)
        sel = _rank_select(gate, nblk, nblk, 1)
        s = s + jnp.concatenate([jnp.broadcast_to(sel[:, n:n + 1], (sel.shape[0], ppb * page)) for n in range(nblk)],
                                axis=1)
    m = jnp.maximum(jnp.max(s, axis=1, keepdims=True), s_new)
    p = jnp.exp(s - m)
    p_new = jnp.exp(s_new - m)
    l = jnp.sum(p, axis=1, keepdims=True) + p_new
    acc = p_new * vn
    for i in range(npages):
        acc = acc + jnp.dot(p[:, i * page:(i + 1) * page].astype(bf16), _page_rows(v_refs[i], kvh).astype(bf16),
                            preferred_element_type=f32)
    o_ref[0] = _own_kv_slab(acc / l, kvh)


def _decode_attention(kind, q, k_pool, v_pool, lf_pool, k_new, v_new, lf_new, page_table, *, kvh, hd):
    db, npages = page_table.shape
    heads = q.shape[1] // hd
    kvw = kvh * hd
    page = k_pool.shape[1]
    scale = hd ** -0.5
    ppb = MOBA_BLOCK // page
    assert npages % ppb == 0 and 2 * (npages // ppb) <= LANES
    q3 = q.reshape(db, heads, hd)
    k_flat = k_pool.reshape(k_pool.shape[0], page * kvh, hd)
    v_flat = v_pool.reshape(v_pool.shape[0], page * kvh, hd)

    def paged_kv(i):
        return pl.BlockSpec((1, page * kvh, hd), lambda b, pt: (pt[b * npages + i], 0, 0))

    per_seq = lambda shape: pl.BlockSpec((1,) + shape, lambda b, pt: (b, 0, 0))
    in_specs = [per_seq((heads, hd))] + [paged_kv(i) for i in range(npages)] * 2
    args = [q3] + [k_flat] * npages + [v_flat] * npages
    if kind == "fox":
        lf_t = jnp.swapaxes(lf_pool, 1, 2)
        in_specs += [pl.BlockSpec((1, heads, page), lambda b, pt, i=i: (pt[b * npages + i], 0, 0))
                     for i in range(npages)]
        args += [lf_t] * npages
        in_specs += [per_seq((1, kvw)), per_seq((1, kvw)), per_seq((heads, 1))]
        args += [k_new.reshape(db, 1, kvw), v_new.reshape(db, 1, kvw), lf_new.reshape(db, heads, 1)]
    else:
        in_specs += [per_seq((1, kvw)), per_seq((1, kvw))]
        args += [k_new.reshape(db, 1, kvw), v_new.reshape(db, 1, kvw)]
    out = pl.pallas_call(
        functools.partial(_decode_kernel, kind=kind, npages=npages, kvh=kvh, ppb=ppb, scale=scale),
        grid_spec=pltpu.PrefetchScalarGridSpec(
            num_scalar_prefetch=1,
            grid=(db,),
            in_specs=in_specs,
            out_specs=pl.BlockSpec((1, heads, hd), lambda b, pt: (b, 0, 0)),
        ),
        out_shape=jax.ShapeDtypeStruct((db, heads, hd), f32),
        compiler_params=_params(("arbitrary",), [((page, kvw), f32)] * (2 * npages),
                                6 * npages * _nbytes((page, kvw), bf16)),
        name=kind + "_decode",
    )(page_table.reshape(-1), *args)
    return out.reshape(db, heads * hd)


def _rope_tables(pos, hd):
    half = hd // 2
    inv_freq = ROPE_THETA ** (-jnp.arange(half, dtype=f32) / half)
    ang = pos.astype(f32)[:, None] * inv_freq[None, :]
    cos, sin = jnp.cos(ang), jnp.sin(ang)
    return jnp.concatenate([cos, cos], axis=-1), jnp.concatenate([-sin, sin], axis=-1)


def kernel(x_prompt, x_sample, cache_fox_k, cache_fox_v, cache_fox_logf, cache_moba_k, cache_moba_v, page_table,
           g_pre_ffn1, g_post_ffn1, w1_gate, w1_up, w1_down, g_pre_mix, w_in, b_forget, g_out_fox, g_out_moba,
           w_out, g_post_mix, g_pre_ffn2, g_post_ffn2, w2_gate, w2_up, w2_down):
    depth = w_in.shape[0]
    assert depth == 1, "one layer: the sample group reads the caches the prompt group does not write"
    batch, seq, d = x_prompt.shape
    db, dseq, _ = x_sample.shape
    assert dseq == 1
    kvh, hd = cache_fox_k.shape[3], cache_fox_k.shape[4]
    assert hd == LANES
    kvw = kvh * hd
    fh = b_forget.shape[-1]
    fw = fh * hd
    mw = w_in.shape[-1] - fw - 4 * kvw - fh
    page = cache_fox_k.shape[2]
    past_len = page_table.shape[1] * page
    assert seq % MOBA_BLOCK == 0 and past_len % MOBA_BLOCK == 0
    tp = batch * seq

    def layer0(a):
        return a.reshape(a.shape[1:])

    def half_ffn(res_parts, hn, g_post, w_gate, w_up, w_down, **kw):
        a = _gate_up(hn, layer0(w_gate), layer0(w_up))
        y = _matmul(a, layer0(w_down), MM_COLS)
        return _resnorm(y, res_parts, layer0(g_post), 0.5, **kw)

    x_parts = [x_prompt.reshape(tp, d), x_sample.reshape(db, d)]
    hn = _prenorm(x_parts, layer0(g_pre_ffn1))
    h, hn = half_ffn(x_parts, hn, g_post_ffn1, w1_gate, w1_up, w1_down, g_next=layer0(g_pre_mix))

    pos = jnp.concatenate([jnp.tile(jnp.arange(seq, dtype=jnp.int32), batch), jnp.full((db,), past_len, jnp.int32)])
    cosf, sinf = _rope_tables(pos, hd)
    w_in_t = jnp.swapaxes(layer0(w_in), 0, 1)
    qf, kf, vf, qm, km, vm, lf = _proj(hn, w_in_t, cosf, sinf, layer0(b_forget), fw=fw, kvw=kvw, mw=mw, fh=fh, hd=hd)

    scale = hd ** -0.5
    caug = _cumsum_logf(lf, batch=batch, seq=seq, kvh=kvh, scale=scale)
    of_p = _prompt_attention("fox", qf, kf, vf, caug, batch=batch, seq=seq, kvh=kvh, hd=hd)
    om_p = _prompt_attention("moba", qm, km, vm, None, batch=batch, seq=seq, kvh=kvh, hd=hd)
    of_s = _decode_attention("fox", qf[tp:], layer0(cache_fox_k), layer0(cache_fox_v), layer0(cache_fox_logf),
                             kf[tp * kvh:], vf[tp * kvh:], lf[tp:], page_table, kvh=kvh, hd=hd)
    om_s = _decode_attention("moba", qm[tp:], layer0(cache_moba_k), layer0(cache_moba_v), None,
                             km[tp * kvh:], vm[tp * kvh:], None, page_table, kvh=kvh, hd=hd)

    z = _groupnorm([of_p, of_s], [om_p, om_s], layer0(g_out_fox), layer0(g_out_moba))
    y = _matmul(z, layer0(w_out), MM_COLS_WIDE)
    h, hn = _resnorm(y, [h], layer0(g_post_mix), 1.0, g_next=layer0(g_pre_ffn2))
    yp, ys = half_ffn([h], hn, g_post_ffn2, w2_gate, w2_up, w2_down, out_rows=[tp, db])

    def rows(a, lo, b, s):
        return a[lo * kvh:(lo + b * s) * kvh].reshape(depth, b, s, kvh, hd)

    return (yp.reshape(batch, seq, d), ys.reshape(db, dseq, d),
            rows(kf, 0, batch, seq), rows(vf, 0, batch, seq), lf[:tp].reshape(depth, batch, seq, fh),
            rows(km, 0, batch, seq), rows(vm, 0, batch, seq),
            rows(kf, tp, db, dseq), rows(vf, tp, db, dseq), lf[tp:].reshape(depth, db, dseq, fh),
            rows(km, tp, db, dseq), rows(vm, tp, db, dseq))
```

```python
import functools
import math

import jax
import jax.numpy as jnp
from jax import lax
from jax.experimental import pallas as pl
from jax.experimental.pallas import tpu as pltpu

f32 = jnp.float32
bf16 = jnp.bfloat16

MOBA_BLOCK = 256
MOBA_TOPK = 3
ROPE_THETA = 10000.0
RMS_EPS = 1e-6
LANES = 128
SUBLANES = 8
BF16_ROWS = 16
ROW_TILE = 256
MM_ROWS = 1664
MM_ROWS_WIDE_K = 1040
PROJ_ROWS = 1040
MM_COLS = 256
MM_COLS_WIDE = 512
VMEM_CAP = 60 * 1024 * 1024

_NT = (((1,), (1,)), ((), ()))


def _nbytes(shape, dtype):
    n = 1
    for s in shape:
        n *= s
    return n * jnp.dtype(dtype).itemsize


def _params(sem, pipelined, resident=0):
    est = 2 * sum(_nbytes(s, d) for s, d in pipelined) + resident
    limit = min(VMEM_CAP, max(32 * 1024 * 1024, est + est // 4))
    return pltpu.CompilerParams(dimension_semantics=sem, vmem_limit_bytes=int(limit))


def _divisor_tile(t, cap, mult):
    best = None
    for c in range(mult, min(t, cap) + 1, mult):
        if t % c == 0:
            best = c
    assert best is not None, (t, cap, mult)
    return best


def _rmsnorm(x, g):
    ms = jnp.mean(x * x, axis=-1, keepdims=True)
    return x * lax.rsqrt(ms + RMS_EPS) * g


def _part_tiles(parts, tr):
    spans, start = [], 0
    for n, a in enumerate(parts):
        assert a.shape[0] % tr == 0 or n == len(parts) - 1
        count = pl.cdiv(a.shape[0], tr)
        spans.append((start, count))
        start += count
    return spans, start


def _part_spec(span, rows, tr, width):
    first, count = span
    return pl.BlockSpec((min(tr, rows), width), lambda i: (jnp.clip(i - first, 0, count - 1), 0))


def _read_parts(refs, spans, tr):
    i = pl.program_id(0)
    x = None
    for r, (first, _) in zip(refs, spans):
        v = r[...]
        if v.shape[0] < tr:
            v = jnp.concatenate([v, jnp.zeros((tr - v.shape[0], v.shape[1]), v.dtype)], axis=0)
        x = v if x is None else jnp.where(i >= first, v, x)
    return x


def _write_parts(refs, spans, x):
    i = pl.program_id(0)
    for r, (first, count) in zip(refs, spans):
        @pl.when((i >= first) & (i < first + count))
        def _(r=r):
            r[...] = x[:r.shape[0]].astype(r.dtype)


def _row_tile(parts):
    return min(ROW_TILE, sum(a.shape[0] for a in parts))


def _prenorm_kernel(*refs, spans):
    n = len(spans)
    g_ref, o_ref = refs[n], refs[n + 1]
    o_ref[...] = _rmsnorm(_read_parts(refs[:n], spans, o_ref.shape[0]), g_ref[...]).astype(o_ref.dtype)


def _prenorm(x_parts, g):
    d = x_parts[0].shape[1]
    tr = _row_tile(x_parts)
    spans, tiles = _part_tiles(x_parts, tr)
    return pl.pallas_call(
        functools.partial(_prenorm_kernel, spans=spans),
        grid=(tiles,),
        in_specs=[_part_spec(s, a.shape[0], tr, d) for s, a in zip(spans, x_parts)]
        + [pl.BlockSpec((1, d), lambda i: (0, 0))],
        out_specs=pl.BlockSpec((tr, d), lambda i: (i, 0)),
        out_shape=jax.ShapeDtypeStruct((sum(a.shape[0] for a in x_parts), d), bf16),
        compiler_params=_params(("arbitrary",), [((tr, d), f32)] * len(spans) + [((tr, d), bf16)],
                                2 * _nbytes((tr, d), f32)),
        name="prenorm",
    )(*x_parts, g.reshape(1, d))


def _resnorm_kernel(*refs, res_spans, out_spans, weight, has_next):
    y_ref = refs[0]
    nres = len(res_spans)
    res_refs = refs[1:1 + nres]
    gp_ref = refs[1 + nres]
    rest = refs[2 + nres:]
    y = _rmsnorm(y_ref[...], gp_ref[...])
    if weight != 1.0:
        y = weight * y
    h = _read_parts(res_refs, res_spans, y.shape[0]) + y
    if has_next:
        gn_ref, h_ref, hn_ref = rest
        h_ref[...] = h
        hn_ref[...] = _rmsnorm(h, gn_ref[...]).astype(hn_ref.dtype)
    else:
        _write_parts(rest, out_spans, h)


def _resnorm(y, res_parts, g_post, weight, g_next=None, out_rows=None):
    t, d = y.shape
    tr = _row_tile(res_parts)
    res_spans, tiles = _part_tiles(res_parts, tr)
    assert sum(a.shape[0] for a in res_parts) == t
    row = pl.BlockSpec((tr, d), lambda i: (i, 0))
    vec = pl.BlockSpec((1, d), lambda i: (0, 0))
    has_next = g_next is not None
    in_specs = ([row] + [_part_spec(s, a.shape[0], tr, d) for s, a in zip(res_spans, res_parts)] + [vec]
                + ([vec] if has_next else []))
    args = [y, *res_parts, g_post.reshape(1, d)] + ([g_next.reshape(1, d)] if has_next else [])
    if has_next:
        out_spans = None
        out_specs = [row, row]
        out_shape = [jax.ShapeDtypeStruct((t, d), f32), jax.ShapeDtypeStruct((t, d), bf16)]
    else:
        out_rows = out_rows or [t]
        assert sum(out_rows) == t
        out_shape = [jax.ShapeDtypeStruct((r, d), f32) for r in out_rows]
        out_spans, _ = _part_tiles(out_shape, tr)
        out_specs = [_part_spec(s, r, tr, d) for s, r in zip(out_spans, out_rows)]
    return pl.pallas_call(
        functools.partial(_resnorm_kernel, res_spans=res_spans, out_spans=out_spans, weight=weight, has_next=has_next),
        grid=(tiles,),
        in_specs=in_specs,
        out_specs=out_specs,
        out_shape=out_shape,
        compiler_params=_params(("arbitrary",), [((tr, d), f32)] * 5, 3 * _nbytes((tr, d), f32)),
        name="resnorm",
    )(*args)


def _groupnorm_kernel(*refs, spans):
    n = len(spans)
    of_refs, om_refs = refs[:n], refs[n:2 * n]
    gf_ref, gm_ref, z_ref = refs[2 * n:]
    fw = of_refs[0].shape[1]
    tr = z_ref.shape[0]
    z_ref[:, :fw] = _rmsnorm(_read_parts(of_refs, spans, tr), gf_ref[...]).astype(z_ref.dtype)
    z_ref[:, fw:] = _rmsnorm(_read_parts(om_refs, spans, tr), gm_ref[...]).astype(z_ref.dtype)


def _groupnorm(of_parts, om_parts, g_f, g_m):
    fw, mw = of_parts[0].shape[1], om_parts[0].shape[1]
    tr = _row_tile(of_parts)
    spans, tiles = _part_tiles(of_parts, tr)
    return pl.pallas_call(
        functools.partial(_groupnorm_kernel, spans=spans),
        grid=(tiles,),
        in_specs=[_part_spec(s, a.shape[0], tr, fw) for s, a in zip(spans, of_parts)]
        + [_part_spec(s, a.shape[0], tr, mw) for s, a in zip(spans, om_parts)]
        + [pl.BlockSpec((1, fw), lambda i: (0, 0)), pl.BlockSpec((1, mw), lambda i: (0, 0))],
        out_specs=pl.BlockSpec((tr, fw + mw), lambda i: (i, 0)),
        out_shape=jax.ShapeDtypeStruct((sum(a.shape[0] for a in of_parts), fw + mw), bf16),
        compiler_params=_params(("arbitrary",), [((tr, fw + mw), f32)] * len(spans) + [((tr, fw + mw), bf16)],
                                2 * _nbytes((tr, fw + mw), f32)),
        name="groupnorm",
    )(*of_parts, *om_parts, g_f.reshape(1, fw), g_m.reshape(1, mw))


def _prefix_lanes(x):
    lane = lax.broadcasted_iota(jnp.int32, x.shape, 1)
    sh = 1
    while sh < x.shape[1]:
        x = x + jnp.where(lane >= sh, pltpu.roll(x, sh, 1), 0.0)
        sh *= 2
    return x


def _cumsum_kernel(lf_ref, c_ref, *, group, inv_scale):
    c = _prefix_lanes(lf_ref[...].T)
    bt = (c * (-inv_scale)).T
    pad = jnp.zeros((bt.shape[0], LANES - 3 * group), f32)
    for kv in range(c_ref.shape[1]):
        b = bt[:, kv * group:(kv + 1) * group]
        hi = b.astype(bf16).astype(f32)
        mid = (b - hi).astype(bf16).astype(f32)
        lo = b - hi - mid
        c_ref[0, kv] = jnp.concatenate([hi, mid, lo, pad], axis=1).astype(bf16)


def _cumsum_logf(logf, *, batch, seq, kvh, scale):
    fh = logf.shape[1]
    g = fh // kvh
    assert 3 * g <= LANES
    return pl.pallas_call(
        functools.partial(_cumsum_kernel, group=g, inv_scale=1.0 / scale),
        grid=(batch,),
        in_specs=[pl.BlockSpec((seq, fh), lambda i: (i, 0))],
        out_specs=pl.BlockSpec((1, kvh, seq, LANES), lambda i: (i, 0, 0, 0)),
        out_shape=jax.ShapeDtypeStruct((batch, kvh, seq, LANES), bf16),
        compiler_params=_params(("arbitrary",), [((seq, LANES), f32), ((kvh * seq, LANES), bf16)],
                                8 * _nbytes((seq, LANES), f32)),
        name="cumsum_logf",
    )(logf)


def _gate_up_kernel(x_ref, wg_ref, wu_ref, o_ref):
    x = x_ref[...]
    g = jnp.dot(x, wg_ref[...].astype(bf16), preferred_element_type=f32)
    u = jnp.dot(x, wu_ref[...].astype(bf16), preferred_element_type=f32)
    o_ref[...] = (jax.nn.silu(g) * u).astype(o_ref.dtype)


def _gate_up(xn, w_gate, w_up):
    t, d = xn.shape
    ff = w_gate.shape[1]
    tm = _divisor_tile(t, MM_ROWS, BF16_ROWS)
    tn = MM_COLS
    return pl.pallas_call(
        _gate_up_kernel,
        grid=(t // tm, pl.cdiv(ff, tn)),
        in_specs=[pl.BlockSpec((tm, d), lambda i, j: (i, 0)),
                  pl.BlockSpec((d, tn), lambda i, j: (0, j)),
                  pl.BlockSpec((d, tn), lambda i, j: (0, j))],
        out_specs=pl.BlockSpec((tm, tn), lambda i, j: (i, j)),
        out_shape=jax.ShapeDtypeStruct((t, ff), bf16),
        compiler_params=_params(("arbitrary", "arbitrary"),
                                [((tm, d), bf16), ((d, tn), f32), ((d, tn), f32), ((tm, tn), bf16)],
                                2 * _nbytes((d, tn), bf16) + 4 * _nbytes((tm, tn), f32)),
        name="gate_up",
    )(xn, w_gate, w_up)


def _matmul_kernel(x_ref, w_ref, o_ref):
    o_ref[...] = jnp.dot(x_ref[...], w_ref[...].astype(bf16), preferred_element_type=f32)


def _matmul(x, w, tn):
    t, k = x.shape
    n = w.shape[1]
    wide = k * tn > 4096 * MM_COLS_WIDE
    tm = _divisor_tile(t, MM_ROWS_WIDE_K if wide else MM_ROWS, BF16_ROWS)
    lhs_mode = dict(pipeline_mode=pl.Buffered(1)) if wide else {}
    return pl.pallas_call(
        _matmul_kernel,
        grid=(t // tm, pl.cdiv(n, tn)),
        in_specs=[pl.BlockSpec((tm, k), lambda i, j: (i, 0), **lhs_mode),
                  pl.BlockSpec((k, tn), lambda i, j: (0, j))],
        out_specs=pl.BlockSpec((tm, tn), lambda i, j: (i, j)),
        out_shape=jax.ShapeDtypeStruct((t, n), f32),
        compiler_params=_params(("arbitrary", "arbitrary"),
                                [((k, tn), f32), ((tm, tn), f32)],
                                (1 if wide else 2) * _nbytes((tm, k), bf16) + _nbytes((k, tn), bf16)
                                + 2 * _nbytes((tm, tn), f32)),
        name="matmul",
    )(x, w)


def _proj_kernel(x_ref, wt_ref, cos_ref, sin_ref, b_ref, *out_refs, bounds, hd, fh, kvh):
    j = pl.program_id(1)

    def product(cols=None):
        w = wt_ref[...] if cols is None else wt_ref[:cols, :]
        return lax.dot_general(x_ref[...], w.astype(bf16), _NT, preferred_element_type=f32)

    def rotary(x):
        return x * cos_ref[...] + pltpu.roll(x, hd // 2, 1) * sin_ref[...]

    def query(ref, roped):
        acc = product()
        for c in range(acc.shape[1] // hd):
            x = acc[:, c * hd:(c + 1) * hd]
            ref[:, c * hd:(c + 1) * hd] = (rotary(x) if roped else x).astype(ref.dtype)

    def keyval(ref, head0, roped):
        acc = product()
        for c in range(acc.shape[1] // hd):
            x = acc[:, c * hd:(c + 1) * hd]
            ref[pl.ds(head0 + c, acc.shape[0], stride=kvh), :] = rotary(x) if roped else x

    def forget(ref):
        ref[...] = jax.nn.log_sigmoid(product(LANES)[:, :fh] + b_ref[...])

    qf_ref, kf_ref, vf_ref, qm_ref, km_ref, vm_ref, lf_ref = out_refs
    heads_per_tile = wt_ref.shape[0] // hd
    branches = [(bounds[0], functools.partial(query, qf_ref, False)),
                (bounds[3], functools.partial(query, qm_ref, True)),
                (bounds[6], functools.partial(forget, lf_ref))]
    for sec, ref, roped in ((1, kf_ref, False), (2, vf_ref, False), (4, km_ref, True), (5, vm_ref, False)):
        lo, hi = bounds[sec]
        branches += [((lo + k, lo + k + 1), functools.partial(keyval, ref, k * heads_per_tile, roped))
                     for k in range(hi - lo)]
    for (lo, hi), act in branches:
        @pl.when((j >= lo) & (j < hi))
        def _(act=act):
            act()


def _proj(xn, w_in_t, cosf, sinf, b_forget, *, fw, kvw, mw, fh, hd):
    t, d = xn.shape
    kvh = kvw // hd
    tn = math.gcd(MM_COLS_WIDE, fw, kvw, mw)
    tm = _divisor_tile(t, PROJ_ROWS, BF16_ROWS)
    widths = [fw, kvw, kvw, mw, kvw, kvw]
    bounds, lo = [], 0
    for w in widths:
        bounds.append((lo, lo + w // tn))
        lo += w // tn
    bounds.append((lo, lo + 1))
    assert lo * tn + fh == w_in_t.shape[0] and fh <= LANES <= tn

    def q_spec(lo, hi):
        return pl.BlockSpec((tm, tn), lambda i, j: (i, jnp.clip(j - lo, 0, hi - lo - 1)))

    kv_spec = pl.BlockSpec((tm * kvh, hd), lambda i, j: (i, 0))
    kv_shape = jax.ShapeDtypeStruct((t * kvh, hd), f32)
    out_specs = [q_spec(*bounds[0]), kv_spec, kv_spec, q_spec(*bounds[3]), kv_spec, kv_spec,
                 pl.BlockSpec((tm, fh), lambda i, j: (i, 0))]
    out_shape = [jax.ShapeDtypeStruct((t, fw), bf16), kv_shape, kv_shape, jax.ShapeDtypeStruct((t, mw), bf16),
                 kv_shape, kv_shape, jax.ShapeDtypeStruct((t, fh), f32)]
    return pl.pallas_call(
        functools.partial(_proj_kernel, bounds=bounds, hd=hd, fh=fh, kvh=kvh),
        grid=(t // tm, bounds[-1][1]),
        in_specs=[pl.BlockSpec((tm, d), lambda i, j: (i, 0), pipeline_mode=pl.Buffered(1)),
                  pl.BlockSpec((tn, d), lambda i, j: (j, 0)),
                  pl.BlockSpec((tm, hd), lambda i, j: (i, 0)),
                  pl.BlockSpec((tm, hd), lambda i, j: (i, 0)),
                  pl.BlockSpec((1, fh), lambda i, j: (0, 0))],
        out_specs=out_specs,
        out_shape=out_shape,
        compiler_params=_params(("arbitrary", "arbitrary"),
                                [((tn, d), f32)] + [((tm, kvw), f32)] * 4 + [((tm, tn), f32)] * 2
                                + [((tm, hd), f32)] * 2,
                                _nbytes((tm, d), bf16) + _nbytes((tn, d), bf16) + 3 * _nbytes((tm, tn), f32)),
        name="proj",
    )(xn, w_in_t, cosf, sinf, b_forget.reshape(1, fh))


MAX_MOBA_BLOCKS = LANES // 2


def _rank_select(gate, n_valid, nblk, axis):
    idx = lax.broadcasted_iota(jnp.int32, gate.shape, axis)
    valid = idx < n_valid
    g = jnp.where(valid, gate, -jnp.inf)
    rank = jnp.zeros(gate.shape, f32)
    size = gate.shape[axis]
    assert size >= 2 * nblk
    for d in range(1, nblk):
        lower = pltpu.roll(g, d, axis)
        upper = pltpu.roll(g, size - d, axis)
        rank = rank + (lower >= g).astype(f32) + (upper > g).astype(f32)
    return jnp.where(valid & (rank < MOBA_TOPK), 0.0, -jnp.inf)


def _attn_kernel(*refs, kind, group, kvh, tq, nblk, scale):
    if kind == "fox":
        q_ref, k_ref, v_ref, c_ref, o_ref, kb_ref, vt_ref, qa_ref, m_ref, acc_ref, s_ref = refs
    else:
        q_ref, k_ref, v_ref, o_ref, kb_ref, vt_ref, qa_ref, m_ref, acc_ref, s_ref, km_ref, sel_ref = refs
    kv_head = pl.program_id(1)
    qi = pl.program_id(2)
    hd = k_ref.shape[-1]
    rows = group * tq

    def block_rows(ref, n):
        return ref[pl.ds(n * tq * kvh + kv_head, tq, stride=kvh), :]

    @pl.when(qi == 0)
    def _():
        means = []
        for n in range(nblk):
            kn = block_rows(k_ref, n)
            kb_ref[n, :, :hd] = kn.astype(bf16)
            if kind == "fox":
                kb_ref[n, :, hd:] = c_ref[0, 0, n * tq:(n + 1) * tq, :]
            else:
                means.append(jnp.mean(kn, axis=0, keepdims=True))
            vt_ref[n, :hd, :] = block_rows(v_ref, n).T.astype(bf16)
            vt_ref[n, hd:, :] = jnp.ones((BF16_ROWS, tq), bf16)
        if kind == "moba":
            km_ref[...] = jnp.concatenate(means + [jnp.zeros((LANES - nblk, hd), f32)], axis=0).astype(bf16)

    q = q_ref[...]
    for g in range(group):
        qa_ref[g * tq:(g + 1) * tq, :hd] = q[:, g * hd:(g + 1) * hd]
    lane_q = lax.broadcasted_iota(jnp.int32, (tq, rows), 1)
    if kind == "fox":
        lane = lax.broadcasted_iota(jnp.int32, (rows, hd), 1)
        head = lax.broadcasted_iota(jnp.int32, (rows, hd), 0) // tq
        pick = (lane == head) | (lane == head + group) | (lane == head + 2 * group)
        qa_ref[:, hd:] = jnp.where(pick, 1.0, 0.0).astype(bf16)
    else:
        gate = lax.dot_general(km_ref[...], qa_ref[...], _NT, preferred_element_type=f32)
        sel_ref[...] = _rank_select(gate[:sel_ref.shape[0]], qi, nblk, 0)

    m_ref[...] = jnp.full(m_ref.shape, -jnp.inf, f32)
    acc_ref[...] = jnp.zeros(acc_ref.shape, f32)

    def scores(j, diagonal):
        s = lax.dot_general(kb_ref[j], qa_ref[...], _NT, preferred_element_type=f32) * scale
        if diagonal:
            key = lax.broadcasted_iota(jnp.int32, s.shape, 0)
            s = jnp.where(key <= (lane_q & (tq - 1)), s, -jnp.inf)
        elif kind == "moba":
            s = s + sel_ref[pl.ds(j, 1), :]
        return s

    def absorb(s, j):
        m_prev = m_ref[...]
        m_new = jnp.maximum(m_prev, jnp.max(s, axis=0, keepdims=True))
        alpha = jnp.exp(m_prev - m_new)
        p = jnp.exp(s - m_new).astype(bf16)
        acc_ref[...] = alpha * acc_ref[...] + jnp.dot(vt_ref[j], p, preferred_element_type=f32)
        m_ref[...] = m_new

    def prev_block(t):
        return jnp.where(t == 0, qi, t - 1)

    s_ref[0] = scores(qi, True)

    def pair(i, carry):
        t = 2 * i
        s_ref[1] = scores(t, False)
        absorb(s_ref[0], prev_block(t))
        s_ref[0] = scores(t + 1, False)
        absorb(s_ref[1], t)
        return carry

    lax.fori_loop(0, lax.shift_right_logical(qi, 1), pair, 0)

    @pl.when((qi & 1) == 1)
    def _():
        t = qi - 1
        s_ref[1] = scores(t, False)
        absorb(s_ref[0], prev_block(t))
        absorb(s_ref[1], t)

    @pl.when((qi & 1) == 0)
    def _():
        absorb(s_ref[0], prev_block(qi))

    acc = acc_ref[...]
    o = (acc[:hd] / acc[hd:hd + 1]).T
    for g in range(group):
        o_ref[:, g * hd:(g + 1) * hd] = o[g * tq:(g + 1) * tq, :]


def _prompt_attention(kind, q, k, v, caug, *, batch, seq, kvh, hd):
    qw = q.shape[1]
    group = qw // (kvh * hd)
    tq = MOBA_BLOCK
    assert tq & (tq - 1) == 0
    nq = seq // tq
    rows = group * tq
    scale = hd ** -0.5
    width = 2 * hd if kind == "fox" else hd
    q_spec = pl.BlockSpec((tq, group * hd), lambda b, h, i: (b * nq + i, h))
    kv_spec = pl.BlockSpec((seq * kvh, hd), lambda b, h, i: (b, 0))
    scratch = [pltpu.VMEM((nq, tq, width), bf16), pltpu.VMEM((nq, hd + BF16_ROWS, tq), bf16),
               pltpu.VMEM((rows, width), bf16), pltpu.VMEM((1, rows), f32), pltpu.VMEM((hd + BF16_ROWS, rows), f32),
               pltpu.VMEM((2, tq, rows), f32)]
    in_specs = [q_spec, kv_spec, kv_spec]
    args = [q, k, v]
    if kind == "fox":
        in_specs.append(pl.BlockSpec((1, 1, seq, LANES), lambda b, h, i: (b, h, 0, 0)))
        args.append(caug)
    else:
        assert nq <= MAX_MOBA_BLOCKS
        sel_rows = -(-2 * nq // SUBLANES) * SUBLANES
        scratch += [pltpu.VMEM((LANES, hd), bf16), pltpu.VMEM((sel_rows, rows), f32)]
    return pl.pallas_call(
        functools.partial(_attn_kernel, kind=kind, group=group, kvh=kvh, tq=tq, nblk=nq, scale=scale),
        grid=(batch, kvh, nq),
        in_specs=in_specs,
        out_specs=q_spec,
        out_shape=jax.ShapeDtypeStruct((batch * seq, qw), f32),
        scratch_shapes=scratch,
        compiler_params=_params(("arbitrary", "arbitrary", "arbitrary"),
                                [((tq, group * hd), bf16), ((seq * kvh, hd), f32), ((seq * kvh, hd), f32),
                                 ((tq, group * hd), f32), ((seq, LANES), bf16)],
                                _nbytes((seq, 3 * hd), bf16) + 8 * _nbytes((tq, rows), f32)),
        name=kind + "_attention",
    )(*args)


def _block_diag_q(q, kvh):
    heads, hd = q.shape
    group = heads // kvh
    qt = jnp.concatenate([q] * kvh, axis=1)
    row_kv = lax.broadcasted_iota(jnp.int32, qt.shape, 0) // group
    col_kv = lax.broadcasted_iota(jnp.int32, qt.shape, 1) // hd
    return jnp.where(row_kv == col_kv, qt, jnp.zeros_like(qt))


def _own_kv_slab(x, kvh):
    heads, w = x.shape
    hd = w // kvh
    group = heads // kvh
    row_kv = lax.broadcasted_iota(jnp.int32, (heads, hd), 0) // group
    out = jnp.zeros((heads, hd), x.dtype)
    for g in range(kvh):
        out = out + jnp.where(row_kv == g, x[:, g * hd:(g + 1) * hd], 0.0)
    return out


def _page_rows(ref, kvh):
    page = ref.shape[1] // kvh
    return jnp.concatenate([ref[0, pl.ds(h, page, stride=kvh), :] for h in range(kvh)], axis=1)


def _decode_kernel(pt_ref, qf_ref, qm_ref, *refs, npages, kvh, ppb, scale):
    n = npages
    knf_ref, vnf_ref, lfn_ref, knm_ref, vnm_ref, of_ref, om_ref = refs[5 * n:]
    _decode_one("fox", qf_ref, refs[:n], refs[n:2 * n], refs[2 * n:3 * n], knf_ref, vnf_ref, lfn_ref, of_ref,
                kvh=kvh, ppb=ppb, scale=scale)
    _decode_one("moba", qm_ref, refs[3 * n:4 * n], refs[4 * n:5 * n], None, knm_ref, vnm_ref, None, om_ref,
                kvh=kvh, ppb=ppb, scale=scale)


def _decode_one(kind, q_ref, k_refs, v_refs, lf_refs, kn_ref, vn_ref, lfn_ref, o_ref, *, kvh, ppb, scale):
    npages = len(k_refs)
    page = k_refs[0].shape[1] // kvh
    qbd = _block_diag_q(q_ref[0], kvh)
    kn = kn_ref[0].astype(bf16).astype(f32)
    vn = vn_ref[0].astype(bf16).astype(f32)
    s_new = jnp.sum(qbd.astype(f32) * kn, axis=1, keepdims=True) * scale

    k_pages = [_page_rows(r, kvh) for r in k_refs]
    s = jnp.concatenate([lax.dot_general(qbd, kp.astype(bf16), _NT, preferred_element_type=f32) for kp in k_pages],
                        axis=1) * scale
    if kind == "fox":
        c_in = _prefix_lanes(jnp.concatenate([r[0] for r in lf_refs], axis=1))
        s = s - c_in
        s_new = s_new - (c_in[:, c_in.shape[1] - 1:] + lfn_ref[0])
    else:
        nblk = npages // ppb
        sums = [sum(jnp.sum(k_pages[n * ppb + i], axis=0, keepdims=True) for i in range(ppb)) for n in range(nblk)]
        kmean = jnp.concatenate(sums + [jnp.zeros((LANES - nblk, qbd.shape[1]), f32)], axis=0) / (ppb * page)
        gate = lax.dot_general(qbd, kmean.astype(bf16), _NT, preferred_element_type=f32)
        sel = _rank_select(gate, nblk, nblk, 1)
        s = s + jnp.concatenate([jnp.broadcast_to(sel[:, n:n + 1], (sel.shape[0], ppb * page)) for n in range(nblk)],
                                axis=1)
    m = jnp.maximum(jnp.max(s, axis=1, keepdims=True), s_new)
    p = jnp.exp(s - m)
    p_new = jnp.exp(s_new - m)
    l = jnp.sum(p, axis=1, keepdims=True) + p_new
    acc = p_new * vn
    for i in range(npages):
        acc = acc + jnp.dot(p[:, i * page:(i + 1) * page].astype(bf16), _page_rows(v_refs[i], kvh).astype(bf16),
                            preferred_element_type=f32)
    o_ref[0] = _own_kv_slab(acc / l, kvh)


def _decode_attention(q_fox, q_moba, fox_pools, moba_pools, fox_new, moba_new, page_table, *, kvh, hd):
    db, npages = page_table.shape
    kvw = kvh * hd
    page = fox_pools[0].shape[1]
    scale = hd ** -0.5
    ppb = MOBA_BLOCK // page
    assert npages % ppb == 0 and 2 * (npages // ppb) <= LANES
    fheads, mheads = q_fox.shape[1] // hd, q_moba.shape[1] // hd

    def flat(pool):
        return pool.reshape(pool.shape[0], page * kvh, hd)

    def paged(block):
        return [pl.BlockSpec((1,) + block, lambda b, pt, i=i: (pt[b * npages + i], 0, 0)) for i in range(npages)]

    per_seq = lambda shape: pl.BlockSpec((1,) + shape, lambda b, pt: (b, 0, 0))
    lf_t = jnp.swapaxes(fox_pools[2], 1, 2)
    in_specs = ([per_seq((fheads, hd)), per_seq((mheads, hd))]
                + paged((page * kvh, hd)) * 2 + paged((fheads, page)) + paged((page * kvh, hd)) * 2
                + [per_seq((1, kvw)), per_seq((1, kvw)), per_seq((fheads, 1)), per_seq((1, kvw)), per_seq((1, kvw))])
    args = ([q_fox.reshape(db, fheads, hd), q_moba.reshape(db, mheads, hd)]
            + [flat(fox_pools[0])] * npages + [flat(fox_pools[1])] * npages + [lf_t] * npages
            + [flat(moba_pools[0])] * npages + [flat(moba_pools[1])] * npages
            + [fox_new[0].reshape(db, 1, kvw), fox_new[1].reshape(db, 1, kvw), fox_new[2].reshape(db, fheads, 1),
               moba_new[0].reshape(db, 1, kvw), moba_new[1].reshape(db, 1, kvw)])
    o_fox, o_moba = pl.pallas_call(
        functools.partial(_decode_kernel, npages=npages, kvh=kvh, ppb=ppb, scale=scale),
        grid_spec=pltpu.PrefetchScalarGridSpec(
            num_scalar_prefetch=1,
            grid=(db,),
            in_specs=in_specs,
            out_specs=[per_seq((fheads, hd)), per_seq((mheads, hd))],
        ),
        out_shape=[jax.ShapeDtypeStruct((db, fheads, hd), f32), jax.ShapeDtypeStruct((db, mheads, hd), f32)],
        compiler_params=_params(("arbitrary",), [((page, kvw), f32)] * (4 * npages),
                                6 * npages * _nbytes((page, kvw), bf16)),
        name="decode",
    )(page_table.reshape(-1), *args)
    return o_fox.reshape(db, fheads * hd), o_moba.reshape(db, mheads * hd)


def _rope_tables(pos, hd):
    half = hd // 2
    inv_freq = ROPE_THETA ** (-jnp.arange(half, dtype=f32) / half)
    ang = pos.astype(f32)[:, None] * inv_freq[None, :]
    cos, sin = jnp.cos(ang), jnp.sin(ang)
    return jnp.concatenate([cos, cos], axis=-1), jnp.concatenate([-sin, sin], axis=-1)


def kernel(x_prompt, x_sample, cache_fox_k, cache_fox_v, cache_fox_logf, cache_moba_k, cache_moba_v, page_table,
           g_pre_ffn1, g_post_ffn1, w1_gate, w1_up, w1_down, g_pre_mix, w_in, b_forget, g_out_fox, g_out_moba,
           w_out, g_post_mix, g_pre_ffn2, g_post_ffn2, w2_gate, w2_up, w2_down):
    depth = w_in.shape[0]
    assert depth == 1, "one layer: the sample group reads the caches the prompt group does not write"
    batch, seq, d = x_prompt.shape
    db, dseq, _ = x_sample.shape
    assert dseq == 1
    kvh, hd = cache_fox_k.shape[3], cache_fox_k.shape[4]
    assert hd == LANES
    kvw = kvh * hd
    fh = b_forget.shape[-1]
    fw = fh * hd
    mw = w_in.shape[-1] - fw - 4 * kvw - fh
    page = cache_fox_k.shape[2]
    past_len = page_table.shape[1] * page
    assert seq % MOBA_BLOCK == 0 and past_len % MOBA_BLOCK == 0
    tp = batch * seq

    def layer0(a):
        return a.reshape(a.shape[1:])

    def half_ffn(res_parts, hn, g_post, w_gate, w_up, w_down, **kw):
        a = _gate_up(hn, layer0(w_gate), layer0(w_up))
        y = _matmul(a, layer0(w_down), MM_COLS)
        return _resnorm(y, res_parts, layer0(g_post), 0.5, **kw)

    x_parts = [x_prompt.reshape(tp, d), x_sample.reshape(db, d)]
    hn = _prenorm(x_parts, layer0(g_pre_ffn1))
    h, hn = half_ffn(x_parts, hn, g_post_ffn1, w1_gate, w1_up, w1_down, g_next=layer0(g_pre_mix))

    pos = jnp.concatenate([jnp.tile(jnp.arange(seq, dtype=jnp.int32), batch), jnp.full((db,), past_len, jnp.int32)])
    cosf, sinf = _rope_tables(pos, hd)
    w_in_t = jnp.swapaxes(layer0(w_in), 0, 1)
    qf, kf, vf, qm, km, vm, lf = _proj(hn, w_in_t, cosf, sinf, layer0(b_forget), fw=fw, kvw=kvw, mw=mw, fh=fh, hd=hd)

    scale = hd ** -0.5
    caug = _cumsum_logf(lf, batch=batch, seq=seq, kvh=kvh, scale=scale)
    of_p = _prompt_attention("fox", qf, kf, vf, caug, batch=batch, seq=seq, kvh=kvh, hd=hd)
    om_p = _prompt_attention("moba", qm, km, vm, None, batch=batch, seq=seq, kvh=kvh, hd=hd)
    of_s, om_s = _decode_attention(
        qf[tp:], qm[tp:],
        (layer0(cache_fox_k), layer0(cache_fox_v), layer0(cache_fox_logf)),
        (layer0(cache_moba_k), layer0(cache_moba_v)),
        (kf[tp * kvh:], vf[tp * kvh:], lf[tp:]), (km[tp * kvh:], vm[tp * kvh:]),
        page_table, kvh=kvh, hd=hd)

    z = _groupnorm([of_p, of_s], [om_p, om_s], layer0(g_out_fox), layer0(g_out_moba))
    y = _matmul(z, layer0(w_out), MM_COLS_WIDE)
    h, hn = _resnorm(y, [h], layer0(g_post_mix), 1.0, g_next=layer0(g_pre_ffn2))
    yp, ys = half_ffn([h], hn, g_post_ffn2, w2_gate, w2_up, w2_down, out_rows=[tp, db])

    def rows(a, lo, b, s):
        return a[lo * kvh:(lo + b * s) * kvh].reshape(depth, b, s, kvh, hd)

    return (yp.reshape(batch, seq, d), ys.reshape(db, dseq, d),
            rows(kf, 0, batch, seq), rows(vf, 0, batch, seq), lf[:tp].reshape(depth, batch, seq, fh),
            rows(km, 0, batch, seq), rows(vm, 0, batch, seq),
            rows(kf, tp, db, dseq), rows(vf, tp, db, dseq), lf[tp:].reshape(depth, db, dseq, fh),
            rows(km, tp, db, dseq), rows(vm, tp, db, dseq))
```

```python
import functools
import math

import jax
import jax.numpy as jnp
from jax import lax
from jax.experimental import pallas as pl
from jax.experimental.pallas import tpu as pltpu

f32 = jnp.float32
bf16 = jnp.bfloat16

MOBA_BLOCK = 256
MOBA_TOPK = 3
ROPE_THETA = 10000.0
RMS_EPS = 1e-6
LANES = 128
SUBLANES = 8
BF16_ROWS = 16
ROW_TILE = 256
MM_ROWS = 1664
MM_ROWS_WIDE_K = 1040
PROJ_ROWS = 1040
MM_COLS = 256
MM_COLS_WIDE = 512
VMEM_CAP = 60 * 1024 * 1024

_NT = (((1,), (1,)), ((), ()))


def _nbytes(shape, dtype):
    n = 1
    for s in shape:
        n *= s
    return n * jnp.dtype(dtype).itemsize


def _params(sem, pipelined, resident=0):
    est = 2 * sum(_nbytes(s, d) for s, d in pipelined) + resident
    limit = min(VMEM_CAP, max(32 * 1024 * 1024, est + est // 4))
    return pltpu.CompilerParams(dimension_semantics=sem, vmem_limit_bytes=int(limit))


def _divisor_tile(t, cap, mult):
    best = None
    for c in range(mult, min(t, cap) + 1, mult):
        if t % c == 0:
            best = c
    assert best is not None, (t, cap, mult)
    return best


def _rmsnorm(x, g):
    ms = jnp.mean(x * x, axis=-1, keepdims=True)
    return x * lax.rsqrt(ms + RMS_EPS) * g


def _part_tiles(parts, tr):
    spans, start = [], 0
    for n, a in enumerate(parts):
        assert a.shape[0] % tr == 0 or n == len(parts) - 1
        count = pl.cdiv(a.shape[0], tr)
        spans.append((start, count))
        start += count
    return spans, start


def _part_spec(span, rows, tr, width):
    first, count = span
    return pl.BlockSpec((min(tr, rows), width), lambda i: (jnp.clip(i - first, 0, count - 1), 0))


def _read_parts(refs, spans, tr):
    i = pl.program_id(0)
    x = None
    for r, (first, _) in zip(refs, spans):
        v = r[...]
        if v.shape[0] < tr:
            v = jnp.concatenate([v, jnp.zeros((tr - v.shape[0], v.shape[1]), v.dtype)], axis=0)
        x = v if x is None else jnp.where(i >= first, v, x)
    return x


def _write_parts(refs, spans, x):
    i = pl.program_id(0)
    for r, (first, count) in zip(refs, spans):
        @pl.when((i >= first) & (i < first + count))
        def _(r=r):
            r[...] = x[:r.shape[0]].astype(r.dtype)


def _row_tile(parts):
    return min(ROW_TILE, sum(a.shape[0] for a in parts))


def _prenorm_kernel(*refs, spans):
    n = len(spans)
    g_ref, o_ref = refs[n], refs[n + 1]
    o_ref[...] = _rmsnorm(_read_parts(refs[:n], spans, o_ref.shape[0]), g_ref[...]).astype(o_ref.dtype)


def _prenorm(x_parts, g):
    d = x_parts[0].shape[1]
    tr = _row_tile(x_parts)
    spans, tiles = _part_tiles(x_parts, tr)
    return pl.pallas_call(
        functools.partial(_prenorm_kernel, spans=spans),
        grid=(tiles,),
        in_specs=[_part_spec(s, a.shape[0], tr, d) for s, a in zip(spans, x_parts)]
        + [pl.BlockSpec((1, d), lambda i: (0, 0))],
        out_specs=pl.BlockSpec((tr, d), lambda i: (i, 0)),
        out_shape=jax.ShapeDtypeStruct((sum(a.shape[0] for a in x_parts), d), bf16),
        compiler_params=_params(("arbitrary",), [((tr, d), f32)] * len(spans) + [((tr, d), bf16)],
                                2 * _nbytes((tr, d), f32)),
        name="prenorm",
    )(*x_parts, g.reshape(1, d))


def _resnorm_kernel(*refs, res_spans, out_spans, weight, has_next):
    y_ref = refs[0]
    nres = len(res_spans)
    res_refs = refs[1:1 + nres]
    gp_ref = refs[1 + nres]
    rest = refs[2 + nres:]
    y = _rmsnorm(y_ref[...], gp_ref[...])
    if weight != 1.0:
        y = weight * y
    h = _read_parts(res_refs, res_spans, y.shape[0]) + y
    if has_next:
        gn_ref, h_ref, hn_ref = rest
        h_ref[...] = h
        hn_ref[...] = _rmsnorm(h, gn_ref[...]).astype(hn_ref.dtype)
    else:
        _write_parts(rest, out_spans, h)


def _resnorm(y, res_parts, g_post, weight, g_next=None, out_rows=None):
    t, d = y.shape
    tr = _row_tile(res_parts)
    res_spans, tiles = _part_tiles(res_parts, tr)
    assert sum(a.shape[0] for a in res_parts) == t
    row = pl.BlockSpec((tr, d), lambda i: (i, 0))
    vec = pl.BlockSpec((1, d), lambda i: (0, 0))
    has_next = g_next is not None
    in_specs = ([row] + [_part_spec(s, a.shape[0], tr, d) for s, a in zip(res_spans, res_parts)] + [vec]
                + ([vec] if has_next else []))
    args = [y, *res_parts, g_post.reshape(1, d)] + ([g_next.reshape(1, d)] if has_next else [])
    if has_next:
        out_spans = None
        out_specs = [row, row]
        out_shape = [jax.ShapeDtypeStruct((t, d), f32), jax.ShapeDtypeStruct((t, d), bf16)]
    else:
        out_rows = out_rows or [t]
        assert sum(out_rows) == t
        out_shape = [jax.ShapeDtypeStruct((r, d), f32) for r in out_rows]
        out_spans, _ = _part_tiles(out_shape, tr)
        out_specs = [_part_spec(s, r, tr, d) for s, r in zip(out_spans, out_rows)]
    return pl.pallas_call(
        functools.partial(_resnorm_kernel, res_spans=res_spans, out_spans=out_spans, weight=weight, has_next=has_next),
        grid=(tiles,),
        in_specs=in_specs,
        out_specs=out_specs,
        out_shape=out_shape,
        compiler_params=_params(("arbitrary",), [((tr, d), f32)] * 5, 3 * _nbytes((tr, d), f32)),
        name="resnorm",
    )(*args)


def _groupnorm_kernel(*refs, spans):
    n = len(spans)
    of_refs, om_refs = refs[:n], refs[n:2 * n]
    gf_ref, gm_ref, z_ref = refs[2 * n:]
    fw = of_refs[0].shape[1]
    tr = z_ref.shape[0]
    z_ref[:, :fw] = _rmsnorm(_read_parts(of_refs, spans, tr), gf_ref[...]).astype(z_ref.dtype)
    z_ref[:, fw:] = _rmsnorm(_read_parts(om_refs, spans, tr), gm_ref[...]).astype(z_ref.dtype)


def _groupnorm(of_parts, om_parts, g_f, g_m):
    fw, mw = of_parts[0].shape[1], om_parts[0].shape[1]
    tr = _row_tile(of_parts)
    spans, tiles = _part_tiles(of_parts, tr)
    return pl.pallas_call(
        functools.partial(_groupnorm_kernel, spans=spans),
        grid=(tiles,),
        in_specs=[_part_spec(s, a.shape[0], tr, fw) for s, a in zip(spans, of_parts)]
        + [_part_spec(s, a.shape[0], tr, mw) for s, a in zip(spans, om_parts)]
        + [pl.BlockSpec((1, fw), lambda i: (0, 0)), pl.BlockSpec((1, mw), lambda i: (0, 0))],
        out_specs=pl.BlockSpec((tr, fw + mw), lambda i: (i, 0)),
        out_shape=jax.ShapeDtypeStruct((sum(a.shape[0] for a in of_parts), fw + mw), bf16),
        compiler_params=_params(("arbitrary",), [((tr, fw + mw), f32)] * len(spans) + [((tr, fw + mw), bf16)],
                                2 * _nbytes((tr, fw + mw), f32)),
        name="groupnorm",
    )(*of_parts, *om_parts, g_f.reshape(1, fw), g_m.reshape(1, mw))


def _prefix_lanes(x):
    lane = lax.broadcasted_iota(jnp.int32, x.shape, 1)
    sh = 1
    while sh < x.shape[1]:
        x = x + jnp.where(lane >= sh, pltpu.roll(x, sh, 1), 0.0)
        sh *= 2
    return x


def _cumsum_kernel(lf_ref, c_ref, *, group, inv_scale):
    c = _prefix_lanes(lf_ref[...].T)
    bt = (c * (-inv_scale)).T
    pad = jnp.zeros((bt.shape[0], LANES - 3 * group), f32)
    for kv in range(c_ref.shape[1]):
        b = bt[:, kv * group:(kv + 1) * group]
        hi = b.astype(bf16).astype(f32)
        mid = (b - hi).astype(bf16).astype(f32)
        lo = b - hi - mid
        c_ref[0, kv] = jnp.concatenate([hi, mid, lo, pad], axis=1).astype(bf16)


def _cumsum_logf(logf, *, batch, seq, kvh, scale):
    fh = logf.shape[1]
    g = fh // kvh
    assert 3 * g <= LANES
    return pl.pallas_call(
        functools.partial(_cumsum_kernel, group=g, inv_scale=1.0 / scale),
        grid=(batch,),
        in_specs=[pl.BlockSpec((seq, fh), lambda i: (i, 0))],
        out_specs=pl.BlockSpec((1, kvh, seq, LANES), lambda i: (i, 0, 0, 0)),
        out_shape=jax.ShapeDtypeStruct((batch, kvh, seq, LANES), bf16),
        compiler_params=_params(("arbitrary",), [((seq, LANES), f32), ((kvh * seq, LANES), bf16)],
                                8 * _nbytes((seq, LANES), f32)),
        name="cumsum_logf",
    )(logf)


def _gate_up_kernel(x_ref, wg_ref, wu_ref, o_ref):
    x = x_ref[...]
    g = jnp.dot(x, wg_ref[...].astype(bf16), preferred_element_type=f32)
    u = jnp.dot(x, wu_ref[...].astype(bf16), preferred_element_type=f32)
    o_ref[...] = (jax.nn.silu(g) * u).astype(o_ref.dtype)


def _gate_up(xn, w_gate, w_up):
    t, d = xn.shape
    ff = w_gate.shape[1]
    tm = _divisor_tile(t, MM_ROWS, BF16_ROWS)
    tn = MM_COLS
    return pl.pallas_call(
        _gate_up_kernel,
        grid=(t // tm, pl.cdiv(ff, tn)),
        in_specs=[pl.BlockSpec((tm, d), lambda i, j: (i, 0)),
                  pl.BlockSpec((d, tn), lambda i, j: (0, j)),
                  pl.BlockSpec((d, tn), lambda i, j: (0, j))],
        out_specs=pl.BlockSpec((tm, tn), lambda i, j: (i, j)),
        out_shape=jax.ShapeDtypeStruct((t, ff), bf16),
        compiler_params=_params(("arbitrary", "arbitrary"),
                                [((tm, d), bf16), ((d, tn), f32), ((d, tn), f32), ((tm, tn), bf16)],
                                2 * _nbytes((d, tn), bf16) + 4 * _nbytes((tm, tn), f32)),
        name="gate_up",
    )(xn, w_gate, w_up)


def _matmul_kernel(x_ref, w_ref, o_ref):
    o_ref[...] = jnp.dot(x_ref[...], w_ref[...].astype(bf16), preferred_element_type=f32)


def _matmul(x, w, tn):
    t, k = x.shape
    n = w.shape[1]
    wide = k * tn > 4096 * MM_COLS_WIDE
    tm = _divisor_tile(t, MM_ROWS_WIDE_K if wide else MM_ROWS, BF16_ROWS)
    lhs_mode = dict(pipeline_mode=pl.Buffered(1)) if wide else {}
    return pl.pallas_call(
        _matmul_kernel,
        grid=(t // tm, pl.cdiv(n, tn)),
        in_specs=[pl.BlockSpec((tm, k), lambda i, j: (i, 0), **lhs_mode),
                  pl.BlockSpec((k, tn), lambda i, j: (0, j))],
        out_specs=pl.BlockSpec((tm, tn), lambda i, j: (i, j)),
        out_shape=jax.ShapeDtypeStruct((t, n), f32),
        compiler_params=_params(("arbitrary", "arbitrary"),
                                [((k, tn), f32), ((tm, tn), f32)],
                                (1 if wide else 2) * _nbytes((tm, k), bf16) + _nbytes((k, tn), bf16)
                                + 2 * _nbytes((tm, tn), f32)),
        name="matmul",
    )(x, w)


def _proj_kernel(x_ref, wt_ref, cos_ref, sin_ref, b_ref, *out_refs, bounds, hd, fh, kvh):
    j = pl.program_id(1)

    def product(cols=None):
        w = wt_ref[...] if cols is None else wt_ref[:cols, :]
        return lax.dot_general(x_ref[...], w.astype(bf16), _NT, preferred_element_type=f32)

    def rotary(x):
        return x * cos_ref[...] + pltpu.roll(x, hd // 2, 1) * sin_ref[...]

    def query(ref, roped):
        acc = product()
        for c in range(acc.shape[1] // hd):
            x = acc[:, c * hd:(c + 1) * hd]
            ref[:, c * hd:(c + 1) * hd] = (rotary(x) if roped else x).astype(ref.dtype)

    def keyval(ref, head0, roped):
        acc = product()
        for c in range(acc.shape[1] // hd):
            x = acc[:, c * hd:(c + 1) * hd]
            ref[pl.ds(head0 + c, acc.shape[0], stride=kvh), :] = rotary(x) if roped else x

    def forget(ref):
        ref[...] = jax.nn.log_sigmoid(product(LANES)[:, :fh] + b_ref[...])

    qf_ref, kf_ref, vf_ref, qm_ref, km_ref, vm_ref, lf_ref = out_refs
    heads_per_tile = wt_ref.shape[0] // hd
    branches = [(bounds[0], functools.partial(query, qf_ref, False)),
                (bounds[3], functools.partial(query, qm_ref, True)),
                (bounds[6], functools.partial(forget, lf_ref))]
    for sec, ref, roped in ((1, kf_ref, False), (2, vf_ref, False), (4, km_ref, True), (5, vm_ref, False)):
        lo, hi = bounds[sec]
        branches += [((lo + k, lo + k + 1), functools.partial(keyval, ref, k * heads_per_tile, roped))
                     for k in range(hi - lo)]
    for (lo, hi), act in branches:
        @pl.when((j >= lo) & (j < hi))
        def _(act=act):
            act()


def _proj(xn, w_in_t, cosf, sinf, b_forget, *, fw, kvw, mw, fh, hd):
    t, d = xn.shape
    kvh = kvw // hd
    tn = math.gcd(MM_COLS_WIDE, fw, kvw, mw)
    tm = _divisor_tile(t, PROJ_ROWS, BF16_ROWS)
    widths = [fw, kvw, kvw, mw, kvw, kvw]
    bounds, lo = [], 0
    for w in widths:
        bounds.append((lo, lo + w // tn))
        lo += w // tn
    bounds.append((lo, lo + 1))
    assert lo * tn + fh == w_in_t.shape[0] and fh <= LANES <= tn

    def q_spec(lo, hi):
        return pl.BlockSpec((tm, tn), lambda i, j: (i, jnp.clip(j - lo, 0, hi - lo - 1)))

    kv_spec = pl.BlockSpec((tm * kvh, hd), lambda i, j: (i, 0))
    kv_shape = jax.ShapeDtypeStruct((t * kvh, hd), f32)
    out_specs = [q_spec(*bounds[0]), kv_spec, kv_spec, q_spec(*bounds[3]), kv_spec, kv_spec,
                 pl.BlockSpec((tm, fh), lambda i, j: (i, 0))]
    out_shape = [jax.ShapeDtypeStruct((t, fw), bf16), kv_shape, kv_shape, jax.ShapeDtypeStruct((t, mw), bf16),
                 kv_shape, kv_shape, jax.ShapeDtypeStruct((t, fh), f32)]
    return pl.pallas_call(
        functools.partial(_proj_kernel, bounds=bounds, hd=hd, fh=fh, kvh=kvh),
        grid=(t // tm, bounds[-1][1]),
        in_specs=[pl.BlockSpec((tm, d), lambda i, j: (i, 0), pipeline_mode=pl.Buffered(1)),
                  pl.BlockSpec((tn, d), lambda i, j: (j, 0)),
                  pl.BlockSpec((tm, hd), lambda i, j: (i, 0)),
                  pl.BlockSpec((tm, hd), lambda i, j: (i, 0)),
                  pl.BlockSpec((1, fh), lambda i, j: (0, 0))],
        out_specs=out_specs,
        out_shape=out_shape,
        compiler_params=_params(("arbitrary", "arbitrary"),
                                [((tn, d), f32)] + [((tm, kvw), f32)] * 4 + [((tm, tn), f32)] * 2
                                + [((tm, hd), f32)] * 2,
                                _nbytes((tm, d), bf16) + _nbytes((tn, d), bf16) + 3 * _nbytes((tm, tn), f32)),
        name="proj",
    )(xn, w_in_t, cosf, sinf, b_forget.reshape(1, fh))


MAX_MOBA_BLOCKS = LANES // 2


def _rank_select(gate, n_valid, nblk, axis):
    idx = lax.broadcasted_iota(jnp.int32, gate.shape, axis)
    valid = idx < n_valid
    g = jnp.where(valid, gate, -jnp.inf)
    rank = jnp.zeros(gate.shape, f32)
    size = gate.shape[axis]
    assert size >= 2 * nblk
    for d in range(1, nblk):
        lower = pltpu.roll(g, d, axis)
        upper = pltpu.roll(g, size - d, axis)
        rank = rank + (lower >= g).astype(f32) + (upper > g).astype(f32)
    return jnp.where(valid & (rank < MOBA_TOPK), 0.0, -jnp.inf)


def _attn_kernel(*refs, kind, group, kvh, tq, nblk, scale):
    if kind == "fox":
        q_ref, k_ref, v_ref, c_ref, o_ref, kb_ref, vt_ref, qa_ref, m_ref, acc_ref, s_ref = refs
    else:
        q_ref, k_ref, v_ref, o_ref, kb_ref, vt_ref, qa_ref, m_ref, acc_ref, s_ref, km_ref, sel_ref = refs
    kv_head = pl.program_id(1)
    qi = pl.program_id(2)
    hd = k_ref.shape[-1]
    rows = group * tq

    def block_rows(ref, n):
        return ref[pl.ds(n * tq * kvh + kv_head, tq, stride=kvh), :]

    @pl.when(qi == 0)
    def _():
        means = []
        for n in range(nblk):
            kn = block_rows(k_ref, n)
            kb_ref[n, :, :hd] = kn.astype(bf16)
            if kind == "fox":
                kb_ref[n, :, hd:] = c_ref[0, 0, n * tq:(n + 1) * tq, :]
            else:
                means.append(jnp.mean(kn, axis=0, keepdims=True))
            vt_ref[n, :hd, :] = block_rows(v_ref, n).T.astype(bf16)
            vt_ref[n, hd:, :] = jnp.ones((BF16_ROWS, tq), bf16)
        if kind == "moba":
            km_ref[...] = jnp.concatenate(means + [jnp.zeros((LANES - nblk, hd), f32)], axis=0).astype(bf16)
        else:
            lane = lax.broadcasted_iota(jnp.int32, (rows, hd), 1)
            head = lax.broadcasted_iota(jnp.int32, (rows, hd), 0) // tq
            pick = (lane == head) | (lane == head + group) | (lane == head + 2 * group)
            qa_ref[:, hd:] = jnp.where(pick, 1.0, 0.0).astype(bf16)

    q = q_ref[...]
    for g in range(group):
        qa_ref[g * tq:(g + 1) * tq, :hd] = q[:, g * hd:(g + 1) * hd]
    lane_q = lax.broadcasted_iota(jnp.int32, (tq, rows), 1)
    if kind == "moba":
        gate = lax.dot_general(km_ref[...], qa_ref[...], _NT, preferred_element_type=f32)
        sel_ref[...] = _rank_select(gate[:sel_ref.shape[0]], qi, nblk, 0)

    m_ref[...] = jnp.full(m_ref.shape, -jnp.inf, f32)
    acc_ref[...] = jnp.zeros(acc_ref.shape, f32)

    def scores(j, diagonal):
        s = lax.dot_general(kb_ref[j], qa_ref[...], _NT, preferred_element_type=f32) * scale
        if diagonal:
            key = lax.broadcasted_iota(jnp.int32, s.shape, 0)
            s = jnp.where(key <= (lane_q & (tq - 1)), s, -jnp.inf)
        elif kind == "moba":
            s = s + sel_ref[pl.ds(j, 1), :]
        return s

    def absorb(s, j):
        m_prev = m_ref[...]
        m_new = jnp.maximum(m_prev, jnp.max(s, axis=0, keepdims=True))
        alpha = jnp.exp(m_prev - m_new)
        p = jnp.exp(s - m_new).astype(bf16)
        acc_ref[...] = alpha * acc_ref[...] + jnp.dot(vt_ref[j], p, preferred_element_type=f32)
        m_ref[...] = m_new

    def prev_block(t):
        return jnp.where(t == 0, qi, t - 1)

    s_ref[0] = scores(qi, True)

    def pair(i, carry):
        t = 2 * i
        s_ref[1] = scores(t, False)
        absorb(s_ref[0], prev_block(t))
        s_ref[0] = scores(t + 1, False)
        absorb(s_ref[1], t)
        return carry

    lax.fori_loop(0, lax.shift_right_logical(qi, 1), pair, 0)

    @pl.when((qi & 1) == 1)
    def _():
        t = qi - 1
        s_ref[1] = scores(t, False)
        absorb(s_ref[0], prev_block(t))
        absorb(s_ref[1], t)

    @pl.when((qi & 1) == 0)
    def _():
        absorb(s_ref[0], prev_block(qi))

    acc = acc_ref[...]
    o = (acc[:hd] / acc[hd:hd + 1]).T
    for g in range(group):
        o_ref[:, g * hd:(g + 1) * hd] = o[g * tq:(g + 1) * tq, :]


def _prompt_attention(kind, q, k, v, caug, *, batch, seq, kvh, hd):
    qw = q.shape[1]
    group = qw // (kvh * hd)
    tq = MOBA_BLOCK
    assert tq & (tq - 1) == 0
    nq = seq // tq
    rows = group * tq
    scale = hd ** -0.5
    width = 2 * hd if kind == "fox" else hd
    q_spec = pl.BlockSpec((tq, group * hd), lambda b, h, i: (b * nq + i, h))
    kv_spec = pl.BlockSpec((seq * kvh, hd), lambda b, h, i: (b, 0))
    scratch = [pltpu.VMEM((nq, tq, width), bf16), pltpu.VMEM((nq, hd + BF16_ROWS, tq), bf16),
               pltpu.VMEM((rows, width), bf16), pltpu.VMEM((1, rows), f32), pltpu.VMEM((hd + BF16_ROWS, rows), f32),
               pltpu.VMEM((2, tq, rows), f32)]
    in_specs = [q_spec, kv_spec, kv_spec]
    args = [q, k, v]
    if kind == "fox":
        in_specs.append(pl.BlockSpec((1, 1, seq, LANES), lambda b, h, i: (b, h, 0, 0)))
        args.append(caug)
    else:
        assert nq <= MAX_MOBA_BLOCKS
        sel_rows = -(-2 * nq // SUBLANES) * SUBLANES
        scratch += [pltpu.VMEM((LANES, hd), bf16), pltpu.VMEM((sel_rows, rows), f32)]
    return pl.pallas_call(
        functools.partial(_attn_kernel, kind=kind, group=group, kvh=kvh, tq=tq, nblk=nq, scale=scale),
        grid=(batch, kvh, nq),
        in_specs=in_specs,
        out_specs=q_spec,
        out_shape=jax.ShapeDtypeStruct((batch * seq, qw), f32),
        scratch_shapes=scratch,
        compiler_params=_params(("arbitrary", "arbitrary", "arbitrary"),
                                [((tq, group * hd), bf16), ((seq * kvh, hd), f32), ((seq * kvh, hd), f32),
                                 ((tq, group * hd), f32), ((seq, LANES), bf16)],
                                _nbytes((seq, 3 * hd), bf16) + 8 * _nbytes((tq, rows), f32)),
        name=kind + "_attention",
    )(*args)


def _block_diag_q(q, kvh):
    heads, hd = q.shape
    group = heads // kvh
    qt = jnp.concatenate([q] * kvh, axis=1)
    row_kv = lax.broadcasted_iota(jnp.int32, qt.shape, 0) // group
    col_kv = lax.broadcasted_iota(jnp.int32, qt.shape, 1) // hd
    return jnp.where(row_kv == col_kv, qt, jnp.zeros_like(qt))


def _own_kv_slab(x, kvh):
    heads, w = x.shape
    hd = w // kvh
    group = heads // kvh
    row_kv = lax.broadcasted_iota(jnp.int32, (heads, hd), 0) // group
    out = jnp.zeros((heads, hd), x.dtype)
    for g in range(kvh):
        out = out + jnp.where(row_kv == g, x[:, g * hd:(g + 1) * hd], 0.0)
    return out


def _page_rows(ref, kvh):
    page = ref.shape[1] // kvh
    return jnp.concatenate([ref[0, pl.ds(h, page, stride=kvh), :] for h in range(kvh)], axis=1)


def _decode_kernel(pt_ref, qf_ref, qm_ref, *refs, npages, kvh, ppb, scale):
    n = npages
    knf_ref, vnf_ref, lfn_ref, knm_ref, vnm_ref, of_ref, om_ref = refs[5 * n:]
    _decode_one("fox", qf_ref, refs[:n], refs[n:2 * n], refs[2 * n:3 * n], knf_ref, vnf_ref, lfn_ref, of_ref,
                kvh=kvh, ppb=ppb, scale=scale)
    _decode_one("moba", qm_ref, refs[3 * n:4 * n], refs[4 * n:5 * n], None, knm_ref, vnm_ref, None, om_ref,
                kvh=kvh, ppb=ppb, scale=scale)


def _decode_one(kind, q_ref, k_refs, v_refs, lf_refs, kn_ref, vn_ref, lfn_ref, o_ref, *, kvh, ppb, scale):
    npages = len(k_refs)
    page = k_refs[0].shape[1] // kvh
    qbd = _block_diag_q(q_ref[0], kvh)
    kn = kn_ref[0].astype(bf16).astype(f32)
    vn = vn_ref[0].astype(bf16).astype(f32)
    s_new = jnp.sum(qbd.astype(f32) * kn, axis=1, keepdims=True) * scale

    k_pages = [_page_rows(r, kvh) for r in k_refs]
    s = jnp.concatenate([lax.dot_general(qbd, kp.astype(bf16), _NT, preferred_element_type=f32) for kp in k_pages],
                        axis=1) * scale
    if kind == "fox":
        c_in = _prefix_lanes(jnp.concatenate([r[0] for r in lf_refs], axis=1))
        s = s - c_in
        s_new = s_new - (c_in[:, c_in.shape[1] - 1:] + lfn_ref[0])
    else:
        nblk = npages // ppb
        sums = [sum(jnp.sum(k_pages[n * ppb + i], axis=0, keepdims=True) for i in range(ppb)) for n in range(nblk)]
        kmean = jnp.concatenate(sums + [jnp.zeros((LANES - nblk, qbd.shape[1]), f32)], axis=0) / (ppb * page)
        gate = lax.dot_general(qbd, kmean.astype(bf16), _NT, preferred_element_type=f32)
        sel = _rank_select(gate, nblk, nblk, 1)
        s = s + jnp.concatenate([jnp.broadcast_to(sel[:, n:n + 1], (sel.shape[0], ppb * page)) for n in range(nblk)],
                                axis=1)
    m = jnp.maximum(jnp.max(s, axis=1, keepdims=True), s_new)
    p = jnp.exp(s - m)
    p_new = jnp.exp(s_new - m)
    l = jnp.sum(p, axis=1, keepdims=True) + p_new
    acc = p_new * vn
    for i in range(npages):
        acc = acc + jnp.dot(p[:, i * page:(i + 1) * page].astype(bf16), _page_rows(v_refs[i], kvh).astype(bf16),
                            preferred_element_type=f32)
    o_ref[0] = _own_kv_slab(acc / l, kvh)


def _decode_attention(q_fox, q_moba, fox_pools, moba_pools, fox_new, moba_new, page_table, *, kvh, hd):
    db, npages = page_table.shape
    kvw = kvh * hd
    page = fox_pools[0].shape[1]
    scale = hd ** -0.5
    ppb = MOBA_BLOCK // page
    assert npages % ppb == 0 and 2 * (npages // ppb) <= LANES
    fheads, mheads = q_fox.shape[1] // hd, q_moba.shape[1] // hd

    def flat(pool):
        return pool.reshape(pool.shape[0], page * kvh, hd)

    def paged(block):
        return [pl.BlockSpec((1,) + block, lambda b, pt, i=i: (pt[b * npages + i], 0, 0)) for i in range(npages)]

    per_seq = lambda shape: pl.BlockSpec((1,) + shape, lambda b, pt: (b, 0, 0))
    lf_t = jnp.swapaxes(fox_pools[2], 1, 2)
    in_specs = ([per_seq((fheads, hd)), per_seq((mheads, hd))]
                + paged((page * kvh, hd)) * 2 + paged((fheads, page)) + paged((page * kvh, hd)) * 2
                + [per_seq((1, kvw)), per_seq((1, kvw)), per_seq((fheads, 1)), per_seq((1, kvw)), per_seq((1, kvw))])
    args = ([q_fox.reshape(db, fheads, hd), q_moba.reshape(db, mheads, hd)]
            + [flat(fox_pools[0])] * npages + [flat(fox_pools[1])] * npages + [lf_t] * npages
            + [flat(moba_pools[0])] * npages + [flat(moba_pools[1])] * npages
            + [fox_new[0].reshape(db, 1, kvw), fox_new[1].reshape(db, 1, kvw), fox_new[2].reshape(db, fheads, 1),
               moba_new[0].reshape(db, 1, kvw), moba_new[1].reshape(db, 1, kvw)])
    o_fox, o_moba = pl.pallas_call(
        functools.partial(_decode_kernel, npages=npages, kvh=kvh, ppb=ppb, scale=scale),
        grid_spec=pltpu.PrefetchScalarGridSpec(
            num_scalar_prefetch=1,
            grid=(db,),
            in_specs=in_specs,
            out_specs=[per_seq((fheads, hd)), per_seq((mheads, hd))],
        ),
        out_shape=[jax.ShapeDtypeStruct((db, fheads, hd), f32), jax.ShapeDtypeStruct((db, mheads, hd), f32)],
        compiler_params=_params(("arbitrary",), [((page, kvw), f32)] * (4 * npages),
                                6 * npages * _nbytes((page, kvw), bf16)),
        name="decode",
    )(page_table.reshape(-1), *args)
    return o_fox.reshape(db, fheads * hd), o_moba.reshape(db, mheads * hd)


def _rope_tables(pos, hd):
    half = hd // 2
    inv_freq = ROPE_THETA ** (-jnp.arange(half, dtype=f32) / half)
    ang = pos.astype(f32)[:, None] * inv_freq[None, :]
    cos, sin = jnp.cos(ang), jnp.sin(ang)
    return jnp.concatenate([cos, cos], axis=-1), jnp.concatenate([-sin, sin], axis=-1)


def kernel(x_prompt, x_sample, cache_fox_k, cache_fox_v, cache_fox_logf, cache_moba_k, cache_moba_v, page_table,
           g_pre_ffn1, g_post_ffn1, w1_gate, w1_up, w1_down, g_pre_mix, w_in, b_forget, g_out_fox, g_out_moba,
           w_out, g_post_mix, g_pre_ffn2, g_post_ffn2, w2_gate, w2_up, w2_down):
    depth = w_in.shape[0]
    assert depth == 1, "one layer: the sample group reads the caches the prompt group does not write"
    batch, seq, d = x_prompt.shape
    db, dseq, _ = x_sample.shape
    assert dseq == 1
    kvh, hd = cache_fox_k.shape[3], cache_fox_k.shape[4]
    assert hd == LANES
    kvw = kvh * hd
    fh = b_forget.shape[-1]
    fw = fh * hd
    mw = w_in.shape[-1] - fw - 4 * kvw - fh
    page = cache_fox_k.shape[2]
    past_len = page_table.shape[1] * page
    assert seq % MOBA_BLOCK == 0 and past_len % MOBA_BLOCK == 0
    tp = batch * seq

    def layer0(a):
        return a.reshape(a.shape[1:])

    def half_ffn(res_parts, hn, g_post, w_gate, w_up, w_down, **kw):
        a = _gate_up(hn, layer0(w_gate), layer0(w_up))
        y = _matmul(a, layer0(w_down), MM_COLS)
        return _resnorm(y, res_parts, layer0(g_post), 0.5, **kw)

    x_parts = [x_prompt.reshape(tp, d), x_sample.reshape(db, d)]
    hn = _prenorm(x_parts, layer0(g_pre_ffn1))
    h, hn = half_ffn(x_parts, hn, g_post_ffn1, w1_gate, w1_up, w1_down, g_next=layer0(g_pre_mix))

    pos = jnp.concatenate([jnp.tile(jnp.arange(seq, dtype=jnp.int32), batch), jnp.full((db,), past_len, jnp.int32)])
    cosf, sinf = _rope_tables(pos, hd)
    w_in_t = jnp.swapaxes(layer0(w_in), 0, 1)
    qf, kf, vf, qm, km, vm, lf = _proj(hn, w_in_t, cosf, sinf, layer0(b_forget), fw=fw, kvw=kvw, mw=mw, fh=fh, hd=hd)

    scale = hd ** -0.5
    caug = _cumsum_logf(lf, batch=batch, seq=seq, kvh=kvh, scale=scale)
    of_p = _prompt_attention("fox", qf, kf, vf, caug, batch=batch, seq=seq, kvh=kvh, hd=hd)
    om_p = _prompt_attention("moba", qm, km, vm, None, batch=batch, seq=seq, kvh=kvh, hd=hd)
    of_s, om_s = _decode_attention(
        qf[tp:], qm[tp:],
        (layer0(cache_fox_k), layer0(cache_fox_v), layer0(cache_fox_logf)),
        (layer0(cache_moba_k), layer0(cache_moba_v)),
        (kf[tp * kvh:], vf[tp * kvh:], lf[tp:]), (km[tp * kvh:], vm[tp * kvh:]),
        page_table, kvh=kvh, hd=hd)

    z = _groupnorm([of_p, of_s], [om_p, om_s], layer0(g_out_fox), layer0(g_out_moba))
    y = _matmul(z, layer0(w_out), MM_COLS_WIDE)
    h, hn = _resnorm(y, [h], layer0(g_post_mix), 1.0, g_next=layer0(g_pre_ffn2))
    yp, ys = half_ffn([h], hn, g_post_ffn2, w2_gate, w2_up, w2_down, out_rows=[tp, db])

    def rows(a, lo, b, s):
        return a[lo * kvh:(lo + b * s) * kvh].reshape(depth, b, s, kvh, hd)

    return (yp.reshape(batch, seq, d), ys.reshape(db, dseq, d),
            rows(kf, 0, batch, seq), rows(vf, 0, batch, seq), lf[:tp].reshape(depth, batch, seq, fh),
            rows(km, 0, batch, seq), rows(vm, 0, batch, seq),
            rows(kf, tp, db, dseq), rows(vf, tp, db, dseq), lf[tp:].reshape(depth, db, dseq, fh),
            rows(km, tp, db, dseq), rows(vm, tp, db, dseq))
```
